```python
import math
import jax, jax.numpy as jnp
from jax import lax
import numpy as np

D_MODEL = 1024
BATCH = 8
SEQ = 4096
DEPTH = 1

MLA_HEADS = 16
MLA_Q_RANK = 256
MLA_KV_RANK = 256
MLA_NOPE_DIM = 64
MLA_ROPE_DIM = 32
MLA_QK_DIM = MLA_NOPE_DIM + MLA_ROPE_DIM
MLA_V_DIM = D_MODEL // MLA_HEADS
ROPE_THETA = 10000.0

DIFF_HEADS = 8
DIFF_QK_DIM = 64
DIFF_V_DIM = D_MODEL // DIFF_HEADS

N_EXPERTS = 16
CAPACITY_FACTOR = 2
EXPERT_FF = 2 * D_MODEL

Q_BLOCK = 128
EPS = 1e-6

IN_COLS = (
    MLA_Q_RANK,
    MLA_KV_RANK,
    MLA_ROPE_DIM,
    DIFF_HEADS * 2 * DIFF_QK_DIM,
    DIFF_HEADS * 2 * DIFF_QK_DIM,
    DIFF_HEADS * DIFF_V_DIM,
    D_MODEL,
    D_MODEL,
)
IN_WIDTH = sum(IN_COLS)

kernel_name = "hybrid_mla_diffattn_ec_moe_block"


def _rms_norm(x, w):
    xf = x.astype(jnp.float32)
    y = xf * lax.rsqrt(jnp.mean(xf * xf, axis=-1, keepdims=True) + EPS)
    return (y * w.astype(jnp.float32)).astype(x.dtype)


def _split_columns(z):
    parts, start = [], 0
    for width in IN_COLS:
        parts.append(z[..., start:start + width])
        start += width
    return parts


def _rope_tables(positions):
    inv_freq = 1.0 / (ROPE_THETA ** (jnp.arange(0, MLA_ROPE_DIM, 2, dtype=jnp.float32) / MLA_ROPE_DIM))
    ang = positions.astype(jnp.float32)[..., None] * inv_freq
    return jnp.cos(ang)[:, :, None, :], jnp.sin(ang)[:, :, None, :]


def _apply_rope(x, cos, sin):
    half = x.shape[-1] // 2
    x1, x2 = x[..., :half], x[..., half:]
    c, s = cos.astype(x.dtype), sin.astype(x.dtype)
    return jnp.concatenate([x1 * c - x2 * s, x1 * s + x2 * c], axis=-1)


def _alibi_slopes(n_heads):
    return jnp.asarray([2.0 ** (-8.0 * (i + 1) / n_heads) for i in range(n_heads)], dtype=jnp.float32)


def _to_blocks(a, axis):
    t = a.shape[axis]
    new_shape = a.shape[:axis] + (t // Q_BLOCK, Q_BLOCK) + a.shape[axis + 1:]
    return jnp.moveaxis(a.reshape(new_shape), axis, 0)


def _from_blocks(o, axis):
    o = jnp.moveaxis(o, 0, axis)
    return o.reshape(o.shape[:axis] + (o.shape[axis] * o.shape[axis + 1],) + o.shape[axis + 2:])


def _mla_attention(q, k, v):
    scale = MLA_QK_DIM ** -0.5

    def block(qb):
        s = jnp.einsum('bhqd,bhkd->bhqk', qb, k).astype(jnp.float32) * scale
        p = jax.nn.softmax(s, axis=-1)
        return jnp.einsum('bhqk,bhkd->bhqd', p.astype(v.dtype), v)

    out = lax.map(block, _to_blocks(q, 2))
    return _from_blocks(out, 2)


def _diff_attention(q1, q2, k1, k2, v, lam, slopes, positions):
    scale = DIFF_QK_DIM ** -0.5

    def block(args):
        q1b, q2b, pb = args
        dist = jnp.abs(pb[:, :, None] - positions[:, None, :]).astype(jnp.float32)
        bias = -slopes[None, :, None, None] * dist[:, None, :, :]
        a1 = jax.nn.softmax(jnp.einsum('bhqd,bhkd->bhqk', q1b, k1).astype(jnp.float32) * scale + bias, axis=-1)
        a2 = jax.nn.softmax(jnp.einsum('bhqd,bhkd->bhqk', q2b, k2).astype(jnp.float32) * scale + bias, axis=-1)
        a = a1 - lam * a2
        return jnp.einsum('bhqk,bhkd->bhqd', a.astype(v.dtype), v)

    out = lax.map(block, (_to_blocks(q1, 2), _to_blocks(q2, 2), _to_blocks(positions, 1)))
    return _from_blocks(out, 2)


def _expert_choice_moe(h, router_w, w_gate, w_up, w_down):
    b, t, d = h.shape
    capacity = CAPACITY_FACTOR * t // N_EXPERTS
    logits = jnp.einsum('btd,de->bte', h, router_w).astype(jnp.float32)
    affinity = jax.nn.softmax(logits, axis=-1)
    gate, idx = lax.top_k(jnp.swapaxes(affinity, 1, 2), capacity)
    xe = jax.vmap(lambda hb, ib: hb[ib])(h, idx)
    hidden = jax.nn.silu(jnp.einsum('becd,edf->becf', xe, w_gate)) * jnp.einsum('becd,edf->becf', xe, w_up)
    ye = jnp.einsum('becf,efd->becd', hidden, w_down) * gate[..., None].astype(h.dtype)
    return jax.vmap(
        lambda ib, yb: jnp.zeros((t, d), yb.dtype).at[ib.reshape(-1)].add(yb.reshape(-1, d))
    )(idx, ye)


def _hybrid_layer(x, positions, layer_idx, attn_norm_w, w_in, b_gate,
                  mla_q_norm_w, mla_w_uq, mla_kv_norm_w, mla_w_ukv,
                  mla_q_hnorm_w, mla_k_hnorm_w,
                  diff_q_hnorm_w, diff_k_hnorm_w, diff_lambda, diff_subln_w,
                  w_out, ffn_norm_w, router_w, expert_w_gate, expert_w_up, expert_w_down):
    b, t, _ = x.shape

    h = _rms_norm(x, attn_norm_w)
    z = jnp.einsum('btd,dc->btc', h, w_in)
    c_q, c_kv, k_rope, dq, dk, dv, g_mla, g_diff = _split_columns(z)

    cos, sin = _rope_tables(positions)
    q = jnp.einsum('btr,rc->btc', _rms_norm(c_q, mla_q_norm_w), mla_w_uq).reshape(b, t, MLA_HEADS, MLA_QK_DIM)
    kv = jnp.einsum('btr,rc->btc', _rms_norm(c_kv, mla_kv_norm_w), mla_w_ukv).reshape(
        b, t, MLA_HEADS, MLA_NOPE_DIM + MLA_V_DIM)
    k_nope, v_mla = kv[..., :MLA_NOPE_DIM], kv[..., MLA_NOPE_DIM:]
    k_r = jnp.broadcast_to(k_rope[:, :, None, :], (b, t, MLA_HEADS, MLA_ROPE_DIM))
    k = jnp.concatenate([k_nope, k_r], axis=-1)
    q = _rms_norm(q, mla_q_hnorm_w)
    k = _rms_norm(k, mla_k_hnorm_w)
    q = jnp.concatenate([q[..., :MLA_NOPE_DIM], _apply_rope(q[..., MLA_NOPE_DIM:], cos, sin)], axis=-1)
    k = jnp.concatenate([k[..., :MLA_NOPE_DIM], _apply_rope(k[..., MLA_NOPE_DIM:], cos, sin)], axis=-1)
    o_mla = _mla_attention(q.transpose(0, 2, 1, 3), k.transpose(0, 2, 1, 3), v_mla.transpose(0, 2, 1, 3))
    o_mla = o_mla.transpose(0, 2, 1, 3).reshape(b, t, D_MODEL)

    qd = _rms_norm(dq.reshape(b, t, DIFF_HEADS, 2, DIFF_QK_DIM), diff_q_hnorm_w)
    kd = _rms_norm(dk.reshape(b, t, DIFF_HEADS, 2, DIFF_QK_DIM), diff_k_hnorm_w)
    qd = qd.transpose(0, 2, 3, 1, 4)
    kd = kd.transpose(0, 2, 3, 1, 4)
    vd = dv.reshape(b, t, DIFF_HEADS, DIFF_V_DIM).transpose(0, 2, 1, 3)
    lam_init = 0.8 - 0.6 * math.exp(-0.3 * layer_idx)
    lf = diff_lambda.astype(jnp.float32)
    lam = jnp.exp(jnp.sum(lf[0] * lf[1])) - jnp.exp(jnp.sum(lf[2] * lf[3])) + lam_init
    o_diff = _diff_attention(qd[:, :, 0], qd[:, :, 1], kd[:, :, 0], kd[:, :, 1], vd, lam,
                             _alibi_slopes(DIFF_HEADS), positions)
    o_diff = _rms_norm(o_diff, diff_subln_w) * (1.0 - lam_init)
    o_diff = o_diff.transpose(0, 2, 1, 3).reshape(b, t, D_MODEL)

    gate_a = jax.nn.sigmoid(g_mla + b_gate[:D_MODEL])
    gate_b = jax.nn.sigmoid(g_diff + b_gate[D_MODEL:])
    merged = gate_a * o_mla + gate_b * o_diff
    x = x + jnp.einsum('btc,cd->btd', merged, w_out)

    h2 = _rms_norm(x, ffn_norm_w)
    return x + _expert_choice_moe(h2, router_w, expert_w_gate, expert_w_up, expert_w_down)


def setup_inputs(seed: int = 0) -> dict:
    key = jax.random.key(seed)
    ks = jax.random.split(key, 24)
    f32 = jnp.float32

    def normal(k, shape, fan_in):
        return jax.random.normal(k, shape, f32) * (fan_in ** -0.5)

    def gain(k, shape):
        return 1.0 + 0.02 * jax.random.normal(k, shape, f32)

    x = jax.random.normal(ks[0], (BATCH, SEQ, D_MODEL), f32)
    offsets = jax.random.randint(ks[1], (BATCH, 1), 0, 1024, dtype=jnp.int32)
    positions = offsets + jnp.arange(SEQ, dtype=jnp.int32)[None, :]
    return {
        "x": x,
        "positions": positions,
        "attn_norm_w": gain(ks[2], (DEPTH, D_MODEL)),
        "w_in": normal(ks[3], (DEPTH, D_MODEL, IN_WIDTH), D_MODEL),
        "b_gate": 0.02 * jax.random.normal(ks[4], (DEPTH, 2 * D_MODEL), f32),
        "mla_q_norm_w": gain(ks[5], (DEPTH, MLA_Q_RANK)),
        "mla_w_uq": normal(ks[6], (DEPTH, MLA_Q_RANK, MLA_HEADS * MLA_QK_DIM), MLA_Q_RANK),
        "mla_kv_norm_w": gain(ks[7], (DEPTH, MLA_KV_RANK)),
        "mla_w_ukv": normal(ks[8], (DEPTH, MLA_KV_RANK, MLA_HEADS * (MLA_NOPE_DIM + MLA_V_DIM)), MLA_KV_RANK),
        "mla_q_hnorm_w": gain(ks[9], (DEPTH, MLA_QK_DIM)),
        "mla_k_hnorm_w": gain(ks[10], (DEPTH, MLA_QK_DIM)),
        "diff_q_hnorm_w": gain(ks[11], (DEPTH, DIFF_QK_DIM)),
        "diff_k_hnorm_w": gain(ks[12], (DEPTH, DIFF_QK_DIM)),
        "diff_lambda": 0.1 * jax.random.normal(ks[13], (DEPTH, 4, DIFF_QK_DIM), f32),
        "diff_subln_w": gain(ks[14], (DEPTH, DIFF_V_DIM)),
        "w_out": normal(ks[15], (DEPTH, D_MODEL, D_MODEL), D_MODEL),
        "ffn_norm_w": gain(ks[16], (DEPTH, D_MODEL)),
        "router_w": normal(ks[17], (DEPTH, D_MODEL, N_EXPERTS), D_MODEL),
        "expert_w_gate": normal(ks[18], (DEPTH, N_EXPERTS, D_MODEL, EXPERT_FF), D_MODEL),
        "expert_w_up": normal(ks[19], (DEPTH, N_EXPERTS, D_MODEL, EXPERT_FF), D_MODEL),
        "expert_w_down": normal(ks[20], (DEPTH, N_EXPERTS, EXPERT_FF, D_MODEL), EXPERT_FF),
    }


def reference(x, positions, attn_norm_w, w_in, b_gate, mla_q_norm_w, mla_w_uq, mla_kv_norm_w,
              mla_w_ukv, mla_q_hnorm_w, mla_k_hnorm_w, diff_q_hnorm_w, diff_k_hnorm_w,
              diff_lambda, diff_subln_w, w_out, ffn_norm_w, router_w,
              expert_w_gate, expert_w_up, expert_w_down):
    for l in range(DEPTH):
        x = _hybrid_layer(
            x, positions, l, attn_norm_w[l], w_in[l], b_gate[l],
            mla_q_norm_w[l], mla_w_uq[l], mla_kv_norm_w[l], mla_w_ukv[l],
            mla_q_hnorm_w[l], mla_k_hnorm_w[l],
            diff_q_hnorm_w[l], diff_k_hnorm_w[l], diff_lambda[l], diff_subln_w[l],
            w_out[l], ffn_norm_w[l], router_w[l],
            expert_w_gate[l], expert_w_up[l], expert_w_down[l])
    return x
```

```python
import functools
import math

import jax
import jax.numpy as jnp
from jax import lax
from jax.experimental import pallas as pl
from jax.experimental.pallas import tpu as pltpu

F32 = jnp.float32
MXU_DTYPE = jnp.bfloat16

D_MODEL = 1024
MLA_HEADS = 16
MLA_Q_RANK = 256
MLA_KV_RANK = 256
MLA_NOPE_DIM = 64
MLA_ROPE_DIM = 32
MLA_QK_DIM = MLA_NOPE_DIM + MLA_ROPE_DIM
MLA_V_DIM = D_MODEL // MLA_HEADS
ROPE_THETA = 10000.0
DIFF_HEADS = 8
DIFF_QK_DIM = 64
DIFF_V_DIM = D_MODEL // DIFF_HEADS
N_EXPERTS = 16
CAPACITY_FACTOR = 2
EXPERT_FF = 2 * D_MODEL
EPS = 1e-6

LANES = 128
VMEM_LIMIT = 56 * 1024 * 1024

_SEG_CQ = (0, 256)
_SEG_CKV = (256, 512)
_SEG_KR = (512, 640)
_SEG_DQ = (640, 1664)
_SEG_DK = (1664, 2688)
_SEG_DV = (2688, 3712)
_SEG_GA = (3712, 4736)
_SEG_GB = (4736, 5760)
_IN_WIDTH_PACKED = 5760


def _rms(x, w):
    return x * lax.rsqrt(jnp.mean(x * x, axis=-1, keepdims=True) + EPS) * w


def _sigmoid(x):
    return 1.0 / (1.0 + jnp.exp(-x))


def _fold_lanes(x, op):
    parts = [x[:, i * LANES:(i + 1) * LANES] for i in range(x.shape[1] // LANES)]
    return functools.reduce(op, parts)


def _const_spec(shape):
    return pl.BlockSpec(shape, lambda *_: (0,) * len(shape))


def _proj_kernel(x_ref, anw_ref, win_ref, qnw_ref, wuq_ref, kvnw_ref, wukv_ref, qhw_ref, khw_ref,
                 dqw_ref, dkw_ref, bg_ref, rc_ref, rs1_ref, rs2_ref,
                 q_out, k_out, v_out, qd1_out, qd2_out, kd_out, vd_out, ga_out, gb_out):
    x = x_ref[...]
    hb = _rms(x, anw_ref[...]).astype(MXU_DTYPE)

    def seg(ab):
        return jnp.dot(hb, win_ref[:, ab[0]:ab[1]], preferred_element_type=F32)

    rc, rs1, rs2 = rc_ref[...], rs1_ref[...], rs2_ref[...]

    def rope(v):
        return v * rc + pltpu.roll(v, LANES - 16, 1) * rs1 + pltpu.roll(v, 16, 1) * rs2

    cqn = _rms(seg(_SEG_CQ), qnw_ref[...]).astype(MXU_DTYPE)
    qraw = jnp.dot(cqn, wuq_ref[...], preferred_element_type=F32)
    qhw = qhw_ref[...]
    for h in range(MLA_HEADS):
        qh = qraw[:, h * LANES:(h + 1) * LANES]
        r = lax.rsqrt(jnp.sum(qh * qh, axis=-1, keepdims=True) * (1.0 / MLA_QK_DIM) + EPS)
        q_out[h] = rope(qh * r * qhw).astype(q_out.dtype)

    ckvn = _rms(seg(_SEG_CKV), kvnw_ref[...]).astype(MXU_DTYPE)
    kr = seg(_SEG_KR)
    kraw = jnp.dot(ckvn, wukv_ref[:, :MLA_HEADS * LANES], preferred_element_type=F32)
    v_out[...] = jnp.dot(ckvn, wukv_ref[:, MLA_HEADS * LANES:],
                         preferred_element_type=F32).astype(v_out.dtype)
    khw = khw_ref[...]
    for h in range(MLA_HEADS):
        kh = kraw[:, h * LANES:(h + 1) * LANES] + kr
        r = lax.rsqrt(jnp.sum(kh * kh, axis=-1, keepdims=True) * (1.0 / MLA_QK_DIM) + EPS)
        k_out[h] = rope(kh * r * khw).astype(k_out.dtype)

    lo = lax.broadcasted_iota(jnp.int32, (1, LANES), 1) < DIFF_QK_DIM

    def halfnorm(blk, w):
        sq = blk * blk
        ss_lo = jnp.sum(jnp.where(lo, sq, 0.0), axis=-1, keepdims=True)
        ss_hi = jnp.sum(jnp.where(lo, 0.0, sq), axis=-1, keepdims=True)
        r = jnp.where(lo, lax.rsqrt(ss_lo * (1.0 / DIFF_QK_DIM) + EPS),
                      lax.rsqrt(ss_hi * (1.0 / DIFF_QK_DIM) + EPS))
        return blk * r * w

    dq = seg(_SEG_DQ)
    dqw = dqw_ref[...]
    for h in range(DIFF_HEADS):
        qn = halfnorm(dq[:, h * LANES:(h + 1) * LANES], dqw)
        qd1_out[h] = jnp.where(lo, qn, 0.0).astype(qd1_out.dtype)
        qd2_out[h] = jnp.where(lo, 0.0, qn).astype(qd2_out.dtype)
    dk = seg(_SEG_DK)
    dkw = dkw_ref[...]
    for h in range(DIFF_HEADS):
        kd_out[h] = halfnorm(dk[:, h * LANES:(h + 1) * LANES], dkw).astype(kd_out.dtype)
    vd_out[...] = seg(_SEG_DV).astype(vd_out.dtype)

    bg = bg_ref[...]
    ga_out[...] = _sigmoid(seg(_SEG_GA) + bg[:, :D_MODEL]).astype(ga_out.dtype)
    gb_out[...] = _sigmoid(seg(_SEG_GB) + bg[:, D_MODEL:]).astype(gb_out.dtype)


def _proj_call(x2d, b, t, anw, win, qnw, wuq, kvnw, wukv, qhw, khw, dqw, dkw, bg, rc, rs1, rs2, tm):
    n = b * t
    tpb = t // tm
    row = lambda i: (i, 0)
    head = lambda i: (i // tpb, 0, i % tpb, 0)
    act = lambda: jax.ShapeDtypeStruct((n, D_MODEL), MXU_DTYPE)
    hd = lambda nh: jax.ShapeDtypeStruct((b, nh, t, LANES), MXU_DTYPE)
    return pl.pallas_call(
        _proj_kernel,
        grid=(n // tm,),
        in_specs=[
            pl.BlockSpec((tm, D_MODEL), row),
            _const_spec((1, D_MODEL)),
            _const_spec((D_MODEL, _IN_WIDTH_PACKED)),
            _const_spec((1, MLA_Q_RANK)),
            _const_spec((MLA_Q_RANK, MLA_HEADS * LANES)),
            _const_spec((1, MLA_KV_RANK)),
            _const_spec((MLA_KV_RANK, MLA_HEADS * LANES + D_MODEL)),
            _const_spec((1, LANES)), _const_spec((1, LANES)),
            _const_spec((1, LANES)), _const_spec((1, LANES)),
            _const_spec((1, 2 * D_MODEL)),
            pl.BlockSpec((tm, LANES), row), pl.BlockSpec((tm, LANES), row), pl.BlockSpec((tm, LANES), row),
        ],
        out_specs=[
            pl.BlockSpec((None, MLA_HEADS, tm, LANES), head),
            pl.BlockSpec((None, MLA_HEADS, tm, LANES), head),
            pl.BlockSpec((tm, D_MODEL), row),
            pl.BlockSpec((None, DIFF_HEADS, tm, LANES), head),
            pl.BlockSpec((None, DIFF_HEADS, tm, LANES), head),
            pl.BlockSpec((None, DIFF_HEADS, tm, LANES), head),
            pl.BlockSpec((tm, D_MODEL), row),
            pl.BlockSpec((tm, D_MODEL), row),
            pl.BlockSpec((tm, D_MODEL), row),
        ],
        out_shape=[hd(MLA_HEADS), hd(MLA_HEADS), act(), hd(DIFF_HEADS), hd(DIFF_HEADS), hd(DIFF_HEADS),
                   act(), act(), act()],
        compiler_params=pltpu.CompilerParams(dimension_semantics=("arbitrary",),
                                             vmem_limit_bytes=VMEM_LIMIT),
        name="proj",
    )(x2d, anw, win, qnw, wuq, kvnw, wukv, qhw, khw, dqw, dkw, bg, rc, rs1, rs2)


_NT = (((1,), (1,)), ((), ()))


def _softmax_pv(q, k_ref, v_ref, s_ref, tk, bias_fn=None):
    tq = q.shape[0]
    nk = k_ref.shape[0] // tk

    def scores(kc, mx):
        off = pl.multiple_of(kc * tk, tk)
        s = lax.dot_general(q, k_ref[pl.ds(off, tk), :], _NT, preferred_element_type=F32)
        if bias_fn is not None:
            s = s + bias_fn(off)
        s_ref[:, pl.ds(off, tk)] = s
        return jnp.maximum(mx, _fold_lanes(s, jnp.maximum))

    mx = lax.fori_loop(0, nk, scores, jnp.full((tq, LANES), -jnp.inf, F32))
    m = jnp.max(mx, axis=-1, keepdims=True)

    def weights(kc, carry):
        ls, acc = carry
        off = pl.multiple_of(kc * tk, tk)
        p = jnp.exp(s_ref[:, pl.ds(off, tk)] - m)
        acc = acc + jnp.dot(p.astype(MXU_DTYPE), v_ref[pl.ds(off, tk), :], preferred_element_type=F32)
        return ls + _fold_lanes(p, jnp.add), acc

    ls, acc = lax.fori_loop(0, nk, weights, (jnp.zeros((tq, LANES), F32),
                                              jnp.zeros((tq, v_ref.shape[1]), F32)))
    return acc, jnp.sum(ls, axis=-1, keepdims=True)


def _mla_kernel(q_ref, k_ref, v_ref, o_ref, s_ref, *, tk):
    outs = []
    for h in range(2):
        acc, l = _softmax_pv(q_ref[h], k_ref.at[h], v_ref, s_ref, tk)
        outs.append(acc / l)
    lane = lax.broadcasted_iota(jnp.int32, (1, LANES), 1)
    o_ref[...] = jnp.where(lane < MLA_V_DIM, outs[0], outs[1]).astype(o_ref.dtype)


def _mla_call(q, k, v, tq, tk):
    b, _, t, _ = q.shape
    return pl.pallas_call(
        functools.partial(_mla_kernel, tk=tk),
        grid=(b, MLA_HEADS // 2, t // tq),
        in_specs=[
            pl.BlockSpec((None, 2, tq, LANES), lambda bi, p, i: (bi, p, i, 0)),
            pl.BlockSpec((None, 2, t, LANES), lambda bi, p, i: (bi, p, 0, 0)),
            pl.BlockSpec((None, t, LANES), lambda bi, p, i: (bi, 0, p)),
        ],
        out_specs=pl.BlockSpec((None, tq, LANES), lambda bi, p, i: (bi, i, p)),
        out_shape=jax.ShapeDtypeStruct((b, t, D_MODEL), MXU_DTYPE),
        scratch_shapes=[pltpu.VMEM((tq, t), F32)],
        compiler_params=pltpu.CompilerParams(dimension_semantics=("arbitrary",) * 3,
                                             vmem_limit_bytes=VMEM_LIMIT),
        name="mla_attn",
    )(q, k, v)


def _diff_kernel(sc_ref, q1_ref, q2_ref, k_ref, v_ref, pq_ref, pk_ref, w_ref, o_ref, s_ref, *, tk, out_scale):
    h = pl.program_id(1)
    neg_slope = -sc_ref[h]
    lam = sc_ref[DIFF_HEADS]
    pq = pq_ref[...]

    def bias(off):
        return neg_slope * jnp.abs(pq - pk_ref[:, pl.ds(off, tk)])

    acc1, l1 = _softmax_pv(q1_ref[...], k_ref, v_ref, s_ref, tk, bias)
    acc2, l2 = _softmax_pv(q2_ref[...], k_ref, v_ref, s_ref, tk, bias)
    o = acc1 / l1 - lam * (acc2 / l2)
    o_ref[...] = (_rms(o, w_ref[...]) * out_scale).astype(o_ref.dtype)


def _diff_call(scalars, q1, q2, k, v, posq, posk, subw, tq, tk, out_scale):
    b, _, t, _ = q1.shape
    qspec = pl.BlockSpec((None, None, tq, LANES), lambda bi, h, i: (bi, h, i, 0))
    return pl.pallas_call(
        functools.partial(_diff_kernel, tk=tk, out_scale=out_scale),
        grid=(b, DIFF_HEADS, t // tq),
        in_specs=[
            pl.BlockSpec(memory_space=pltpu.SMEM),
            qspec, qspec,
            pl.BlockSpec((None, None, t, LANES), lambda bi, h, i: (bi, h, 0, 0)),
            pl.BlockSpec((None, t, LANES), lambda bi, h, i: (bi, 0, h)),
            pl.BlockSpec((None, tq, 1), lambda bi, h, i: (bi, i, 0)),
            pl.BlockSpec((None, 1, t), lambda bi, h, i: (bi, 0, 0)),
            pl.BlockSpec((1, LANES), lambda bi, h, i: (0, 0)),
        ],
        out_specs=pl.BlockSpec((None, tq, LANES), lambda bi, h, i: (bi, i, h)),
        out_shape=jax.ShapeDtypeStruct((b, t, D_MODEL), MXU_DTYPE),
        scratch_shapes=[pltpu.VMEM((tq, t), F32)],
        compiler_params=pltpu.CompilerParams(dimension_semantics=("arbitrary",) * 3,
                                             vmem_limit_bytes=VMEM_LIMIT),
        name="diff_attn",
    )(scalars, q1, q2, k, v, posq, posk, subw)


def _split3(a):
    a1 = a.astype(jnp.bfloat16)
    r1 = a - a1.astype(F32)
    a2 = r1.astype(jnp.bfloat16)
    a3 = (r1 - a2.astype(F32)).astype(jnp.bfloat16)
    return a1, a2, a3


def _merge_kernel(x_ref, om_ref, od_ref, ga_ref, gb_ref, wo_ref, fnw_ref, rw1_ref, rw2_ref, rw3_ref,
                  x1_out, aff_out, hp_out):
    merged = (ga_ref[...].astype(F32) * om_ref[...].astype(F32)
              + gb_ref[...].astype(F32) * od_ref[...].astype(F32))
    x1 = x_ref[...] + jnp.dot(merged.astype(MXU_DTYPE), wo_ref[...], preferred_element_type=F32)
    x1_out[...] = x1
    h2 = _rms(x1, fnw_ref[...])

    h1, hh2, h3 = _split3(h2)
    w1, w2, w3 = rw1_ref[...], rw2_ref[...], rw3_ref[...]
    dot = lambda a, w: jnp.dot(a, w, preferred_element_type=F32)
    logits = (dot(h3, w1) + dot(hh2, w2) + dot(h1, w3)) + (dot(hh2, w1) + dot(h1, w2)) + dot(h1, w1)
    lane = lax.broadcasted_iota(jnp.int32, (1, LANES), 1)
    logits = jnp.where(lane < N_EXPERTS, logits, -jnp.inf)
    e = jnp.exp(logits - jnp.max(logits, axis=-1, keepdims=True))
    aff = e / jnp.sum(e, axis=-1, keepdims=True)
    aff_out[...] = aff[:, :N_EXPERTS]

    bits = pltpu.bitcast(h2.astype(jnp.bfloat16).astype(F32), jnp.uint32)
    half = D_MODEL // 2
    hp_out[...] = (bits[:, half:] & jnp.uint32(0xFFFF0000)) | (bits[:, :half] >> 16)


def _merge_call(x2d, om, od, ga, gb, wo, fnw, rw1, rw2, rw3, tm):
    n = x2d.shape[0]
    row = lambda i: (i, 0)
    act = pl.BlockSpec((tm, D_MODEL), row)
    return pl.pallas_call(
        _merge_kernel,
        grid=(n // tm,),
        in_specs=[act, act, act, act, act,
                  _const_spec((D_MODEL, D_MODEL)), _const_spec((1, D_MODEL)),
                  _const_spec((D_MODEL, LANES)), _const_spec((D_MODEL, LANES)), _const_spec((D_MODEL, LANES))],
        out_specs=[act, pl.BlockSpec((tm, N_EXPERTS), row), pl.BlockSpec((tm, D_MODEL // 2), row)],
        out_shape=[jax.ShapeDtypeStruct((n, D_MODEL), F32),
                   jax.ShapeDtypeStruct((n, N_EXPERTS), F32),
                   jax.ShapeDtypeStruct((n, D_MODEL // 2), jnp.uint32)],
        compiler_params=pltpu.CompilerParams(dimension_semantics=("arbitrary",),
                                             vmem_limit_bytes=VMEM_LIMIT),
        name="merge",
    )(x2d, om, od, ga, gb, wo, fnw, rw1, rw2, rw3)


def _route_kernel(a_ref, tri_ref, idx_out, gate_out, rank_ref, *, cap, jc):
    a = a_ref[...]
    ne, t = a.shape
    bits = pltpu.bitcast(a, jnp.int32)

    def search(_, lohi):
        lo, hi = lohi
        mid = lo + ((hi - lo + 1) >> 1)
        cnt = jnp.sum(jnp.where(bits >= mid, 1.0, 0.0), axis=-1, keepdims=True)
        ok = cnt >= cap
        return jnp.where(ok, mid, lo), jnp.where(ok, hi, mid - 1)

    lo0 = jnp.zeros((ne, 1), jnp.int32)
    hi0 = jnp.full((ne, 1), 0x7F800000, jnp.int32)
    thr, _ = lax.fori_loop(0, 32, search, (lo0, hi0))

    tri = tri_ref[...]

    def prefix(mask):
        carry = jnp.zeros((ne, 1), F32)
        cols = []
        for c in range(t // LANES):
            m = mask[:, c * LANES:(c + 1) * LANES]
            cols.append(jnp.dot(m.astype(jnp.bfloat16), tri, preferred_element_type=F32) + carry)
            carry = carry + jnp.sum(m, axis=-1, keepdims=True)
        return jnp.concatenate(cols, axis=1), carry

    gt = (bits > thr).astype(F32)
    eq = (bits == thr).astype(F32)
    pre_eq, _ = prefix(eq)
    need = cap - jnp.sum(gt, axis=-1, keepdims=True)
    sel = jnp.maximum(gt, jnp.where(pre_eq < need, eq, 0.0))
    rank, _ = prefix(sel)
    rank_ref[...] = jnp.where(sel > 0.0, rank, -1.0)

    tok = lax.broadcasted_iota(jnp.int32, (1, t), 1).astype(F32)

    def per_expert(e, _):
        r = rank_ref[pl.ds(e, 1), :]
        av = a_ref[pl.ds(e, 1), :]

        def per_slots(c, _):
            j0 = pl.multiple_of(c * jc, jc)
            slot = (lax.broadcasted_iota(jnp.int32, (jc, 1), 0) + j0).astype(F32)
            hit = r == slot
            idx_out[e, pl.ds(j0, jc), :] = jnp.sum(jnp.where(hit, tok, 0.0), axis=-1,
                                                   keepdims=True).astype(jnp.int32)
            gate_out[e, pl.ds(j0, jc), :] = jnp.sum(jnp.where(hit, av, 0.0), axis=-1, keepdims=True)
            return 0

        return lax.fori_loop(0, cap // jc, per_slots, 0)

    lax.fori_loop(0, ne, per_expert, 0)


def _route_call(aff_t, tri, cap, jc):
    b, ne, t = aff_t.shape
    return pl.pallas_call(
        functools.partial(_route_kernel, cap=cap, jc=jc),
        grid=(b,),
        in_specs=[pl.BlockSpec((None, ne, t), lambda bi: (bi, 0, 0)), _const_spec((LANES, LANES))],
        out_specs=[pl.BlockSpec((None, ne, cap, 1), lambda bi: (bi, 0, 0, 0)),
                   pl.BlockSpec((None, ne, cap, 1), lambda bi: (bi, 0, 0, 0))],
        out_shape=[jax.ShapeDtypeStruct((b, ne, cap, 1), jnp.int32),
                   jax.ShapeDtypeStruct((b, ne, cap, 1), F32)],
        scratch_shapes=[pltpu.VMEM((ne, t), F32)],
        compiler_params=pltpu.CompilerParams(dimension_semantics=("arbitrary",),
                                             vmem_limit_bytes=VMEM_LIMIT),
        name="route",
    )(aff_t, tri)


_ROW_UNROLL = 8


def _expert_kernel(idx_ref, hp_ref, gate_ref, wg_ref, wu_ref, wd_ref, y_out, xe_ref, hid_ref, *, cap):
    def gather(g, _):
        for u in range(_ROW_UNROLL):
            j = g * _ROW_UNROLL + u
            xe_ref[pl.ds(j, 1), :] = hp_ref[pl.ds(idx_ref[0, j], 1), :]
        return 0

    lax.fori_loop(0, cap // _ROW_UNROLL, gather, 0)

    w = xe_ref[...]
    xe = jnp.concatenate([pltpu.bitcast(w << 16, F32),
                          pltpu.bitcast(w & jnp.uint32(0xFFFF0000), F32)], axis=1).astype(MXU_DTYPE)
    fc = 512
    for c in range(EXPERT_FF // fc):
        g = jnp.dot(xe, wg_ref[:, c * fc:(c + 1) * fc], preferred_element_type=F32)
        u = jnp.dot(xe, wu_ref[:, c * fc:(c + 1) * fc], preferred_element_type=F32)
        hid_ref[:, c * fc:(c + 1) * fc] = (g * _sigmoid(g) * u).astype(MXU_DTYPE)
    y = jnp.dot(hid_ref[...], wd_ref[...], preferred_element_type=F32)
    y_out[...] = y * gate_ref[...]


def _expert_call(idx_smem, hp, gate, wg, wu, wd, cap):
    b, t, half = hp.shape
    ne = wg.shape[0]
    return pl.pallas_call(
        functools.partial(_expert_kernel, cap=cap),
        grid=(ne, b),
        in_specs=[
            pl.BlockSpec((None, 1, cap), lambda e, bi: (bi * ne + e, 0, 0), memory_space=pltpu.SMEM),
            pl.BlockSpec((None, t, half), lambda e, bi: (bi, 0, 0)),
            pl.BlockSpec((None, None, cap, 1), lambda e, bi: (bi, e, 0, 0)),
            pl.BlockSpec((None, D_MODEL, EXPERT_FF), lambda e, bi: (e, 0, 0)),
            pl.BlockSpec((None, D_MODEL, EXPERT_FF), lambda e, bi: (e, 0, 0)),
            pl.BlockSpec((None, EXPERT_FF, D_MODEL), lambda e, bi: (e, 0, 0)),
        ],
        out_specs=pl.BlockSpec((None, None, cap, D_MODEL), lambda e, bi: (bi, e, 0, 0)),
        out_shape=jax.ShapeDtypeStruct((b, ne, cap, D_MODEL), F32),
        scratch_shapes=[pltpu.VMEM((cap, half), jnp.uint32), pltpu.VMEM((cap, EXPERT_FF), MXU_DTYPE)],
        compiler_params=pltpu.CompilerParams(dimension_semantics=("arbitrary",) * 2,
                                             vmem_limit_bytes=VMEM_LIMIT),
        name="experts",
    )(idx_smem, hp, gate, wg, wu, wd)


def _combine_kernel(idx_ref, x1_ref, y_ref, o_ref, *, cap):
    @pl.when(pl.program_id(2) == 0)
    def _():
        o_ref[...] = x1_ref[...]

    def scatter(g, _):
        toks = [idx_ref[0, g * _ROW_UNROLL + u] for u in range(_ROW_UNROLL)]
        rows = [o_ref[pl.ds(toks[u], 1), :] + y_ref[pl.ds(g * _ROW_UNROLL + u, 1), :]
                for u in range(_ROW_UNROLL)]
        for u in range(_ROW_UNROLL):
            o_ref[pl.ds(toks[u], 1), :] = rows[u]
        return 0

    lax.fori_loop(0, cap // _ROW_UNROLL, scatter, 0)


def _combine_call(idx_smem, x1, y, cap):
    b, t, d = x1.shape
    ne = y.shape[1]
    half = d // 2
    return pl.pallas_call(
        functools.partial(_combine_kernel, cap=cap),
        grid=(b, 2, ne),
        in_specs=[
            pl.BlockSpec((None, 1, cap), lambda bi, c, e: (bi * ne + e, 0, 0), memory_space=pltpu.SMEM),
            pl.BlockSpec((None, t, half), lambda bi, c, e: (bi, 0, c)),
            pl.BlockSpec((None, None, cap, half), lambda bi, c, e: (bi, e, 0, c)),
        ],
        out_specs=pl.BlockSpec((None, t, half), lambda bi, c, e: (bi, 0, c)),
        out_shape=jax.ShapeDtypeStruct((b, t, d), F32),
        compiler_params=pltpu.CompilerParams(dimension_semantics=("arbitrary",) * 3,
                                             vmem_limit_bytes=VMEM_LIMIT),
        name="combine",
    )(idx_smem, x1, y)


def _pad_heads(w, n_heads, width):
    r = w.shape[0]
    w = w.reshape(r, n_heads, width)
    return jnp.pad(w, ((0, 0), (0, 0), (0, LANES - width))).reshape(r, n_heads * LANES)


def _pack_w_in(w_in):
    o = 0
    parts = {}
    for name, width in (("cq", MLA_Q_RANK), ("ckv", MLA_KV_RANK), ("kr", MLA_ROPE_DIM),
                        ("dq", D_MODEL), ("dk", D_MODEL), ("dv", D_MODEL), ("ga", D_MODEL), ("gb", D_MODEL)):
        parts[name] = w_in[:, o:o + width]
        o += width
    kr = jnp.pad(parts["kr"], ((0, 0), (MLA_NOPE_DIM, LANES - MLA_QK_DIM)))
    return jnp.concatenate([parts["cq"], parts["ckv"], kr, parts["dq"], parts["dk"], parts["dv"],
                            parts["ga"], parts["gb"]], axis=1).astype(MXU_DTYPE)


def _rope_tables(positions):
    half = MLA_ROPE_DIM // 2
    inv_freq = 1.0 / (ROPE_THETA ** (jnp.arange(0, MLA_ROPE_DIM, 2, dtype=F32) / MLA_ROPE_DIM))
    ang = positions.astype(F32).reshape(-1, 1) * inv_freq
    cos, sin = jnp.cos(ang), jnp.sin(ang)
    n = ang.shape[0]
    ones = jnp.ones((n, MLA_NOPE_DIM), F32)
    z = lambda w: jnp.zeros((n, w), F32)
    tail = LANES - MLA_QK_DIM
    rc = jnp.concatenate([ones, cos, cos, z(tail)], axis=1)
    rs1 = jnp.concatenate([z(MLA_NOPE_DIM), -sin, z(half), z(tail)], axis=1)
    rs2 = jnp.concatenate([z(MLA_NOPE_DIM), z(half), sin, z(tail)], axis=1)
    return rc, rs1, rs2


def _pick(t, prefs):
    for p in prefs:
        if t % p == 0:
            return p
    return t


def _layer(x, positions, layer_idx, attn_norm_w, w_in, b_gate, mla_q_norm_w, mla_w_uq, mla_kv_norm_w,
           mla_w_ukv, mla_q_hnorm_w, mla_k_hnorm_w, diff_q_hnorm_w, diff_k_hnorm_w, diff_lambda,
           diff_subln_w, w_out, ffn_norm_w, router_w, expert_w_gate, expert_w_up, expert_w_down):
    b, t, d = x.shape
    n = b * t
    cap = CAPACITY_FACTOR * t // N_EXPERTS
    row = lambda v: v.reshape(1, -1).astype(F32)

    win = _pack_w_in(w_in)
    wuq = _pad_heads(mla_w_uq, MLA_HEADS, MLA_QK_DIM).astype(MXU_DTYPE)
    wukv = mla_w_ukv.reshape(MLA_KV_RANK, MLA_HEADS, MLA_NOPE_DIM + MLA_V_DIM)
    wuk = _pad_heads(wukv[:, :, :MLA_NOPE_DIM].reshape(MLA_KV_RANK, -1), MLA_HEADS, MLA_NOPE_DIM)
    wuv = wukv[:, :, MLA_NOPE_DIM:].reshape(MLA_KV_RANK, -1)
    wukv_p = jnp.concatenate([wuk, wuv], axis=1).astype(MXU_DTYPE)
    pad_gain = lambda w: jnp.pad(w.astype(F32), (0, LANES - MLA_QK_DIM)).reshape(1, LANES)
    qhw = pad_gain(mla_q_hnorm_w) * (MLA_QK_DIM ** -0.5)
    khw = pad_gain(mla_k_hnorm_w)
    dqw = jnp.tile(diff_q_hnorm_w.astype(F32), 2).reshape(1, LANES) * (DIFF_QK_DIM ** -0.5)
    dkw = jnp.tile(diff_k_hnorm_w.astype(F32), 2).reshape(1, LANES)
    rc, rs1, rs2 = _rope_tables(positions)

    tm = _pick(t, (256, 128))
    q, k, v, qd1, qd2, kd, vd, ga, gb = _proj_call(
        x.reshape(n, d), b, t, row(attn_norm_w), win, row(mla_q_norm_w), wuq, row(mla_kv_norm_w), wukv_p,
        qhw, khw, dqw, dkw, row(b_gate), rc, rs1, rs2, tm)

    tq = _pick(t, (512, 256, 128))
    tk = _pick(t, (512, 256, 128))
    o_mla = _mla_call(q, k, v.reshape(b, t, d), tq, tk)

    lam_init = 0.8 - 0.6 * math.exp(-0.3 * layer_idx)
    lf = diff_lambda.astype(F32)
    lam = jnp.exp(jnp.sum(lf[0] * lf[1])) - jnp.exp(jnp.sum(lf[2] * lf[3])) + lam_init
    slopes = jnp.asarray([2.0 ** (-8.0 * (i + 1) / DIFF_HEADS) for i in range(DIFF_HEADS)], F32)
    scalars = jnp.concatenate([slopes, lam.reshape(1)])
    posf = positions.astype(F32)
    o_diff = _diff_call(scalars, qd1, qd2, kd, vd.reshape(b, t, d), posf.reshape(b, t, 1),
                        posf.reshape(b, 1, t), row(diff_subln_w), tq, tk, 1.0 - lam_init)

    rw = jnp.pad(router_w.astype(F32), ((0, 0), (0, LANES - N_EXPERTS)))
    rw1, rw2, rw3 = _split3(rw)
    tm2 = _pick(t, (512, 256, 128))
    x1, aff, hp = _merge_call(x.reshape(n, d), o_mla.reshape(n, d), o_diff.reshape(n, d), ga, gb,
                              w_out.astype(MXU_DTYPE), row(ffn_norm_w), rw1, rw2, rw3, tm2)

    tri = (lax.broadcasted_iota(jnp.int32, (LANES, LANES), 0)
           < lax.broadcasted_iota(jnp.int32, (LANES, LANES), 1)).astype(jnp.bfloat16)
    aff_t = jnp.swapaxes(aff.reshape(b, t, N_EXPERTS), 1, 2)
    idx, gate = _route_call(aff_t, tri, cap, _pick(cap, (32, 16, 8)))
    idx_smem = idx.reshape(b * N_EXPERTS, 1, cap)

    y = _expert_call(idx_smem, hp.reshape(b, t, d // 2), gate, expert_w_gate.astype(MXU_DTYPE),
                     expert_w_up.astype(MXU_DTYPE), expert_w_down.astype(MXU_DTYPE), cap)
    return _combine_call(idx_smem, x1.reshape(b, t, d), y, cap)


def kernel(x, positions, attn_norm_w, w_in, b_gate, mla_q_norm_w, mla_w_uq, mla_kv_norm_w, mla_w_ukv,
           mla_q_hnorm_w, mla_k_hnorm_w, diff_q_hnorm_w, diff_k_hnorm_w, diff_lambda, diff_subln_w, w_out,
           ffn_norm_w, router_w, expert_w_gate, expert_w_up, expert_w_down):
    for l in range(attn_norm_w.shape[0]):
        x = _layer(x, positions, l, attn_norm_w[l], w_in[l], b_gate[l], mla_q_norm_w[l], mla_w_uq[l],
                   mla_kv_norm_w[l], mla_w_ukv[l], mla_q_hnorm_w[l], mla_k_hnorm_w[l], diff_q_hnorm_w[l],
                   diff_k_hnorm_w[l], diff_lambda[l], diff_subln_w[l], w_out[l], ffn_norm_w[l], router_w[l],
                   expert_w_gate[l], expert_w_up[l], expert_w_down[l])
    return x
```

```python
import functools
import math

import jax
import jax.numpy as jnp
from jax import lax
from jax.experimental import pallas as pl
from jax.experimental.pallas import tpu as pltpu

F32 = jnp.float32
MXU_DTYPE = jnp.bfloat16

D_MODEL = 1024
MLA_HEADS = 16
MLA_Q_RANK = 256
MLA_KV_RANK = 256
MLA_NOPE_DIM = 64
MLA_ROPE_DIM = 32
MLA_QK_DIM = MLA_NOPE_DIM + MLA_ROPE_DIM
MLA_V_DIM = D_MODEL // MLA_HEADS
ROPE_THETA = 10000.0
DIFF_HEADS = 8
DIFF_QK_DIM = 64
DIFF_V_DIM = D_MODEL // DIFF_HEADS
N_EXPERTS = 16
CAPACITY_FACTOR = 2
EXPERT_FF = 2 * D_MODEL
EPS = 1e-6
LOG2E = math.log2(math.e)

LANES = 128
VMEM_LIMIT = 56 * 1024 * 1024

_SEG_CQ = (0, 256)
_SEG_CKV = (256, 512)
_SEG_KR = (512, 640)
_SEG_KRR = (640, 768)
_SEG_DQ = (768, 1792)
_SEG_DK = (1792, 2816)
_SEG_DV = (2816, 3840)
_SEG_GA = (3840, 4864)
_SEG_GB = (4864, 5888)
_IN_WIDTH_PACKED = 5888
_HEAD_GROUP = 4


def _rms(x, w):
    return x * lax.rsqrt(jnp.mean(x * x, axis=-1, keepdims=True) + EPS) * w


def _sigmoid(x):
    return 1.0 / (1.0 + jnp.exp(-x))


def _fold_lanes(x, op):
    parts = [x[:, i * LANES:(i + 1) * LANES] for i in range(x.shape[1] // LANES)]
    return functools.reduce(op, parts)


def _const_spec(shape):
    return pl.BlockSpec(shape, lambda *_: (0,) * len(shape), pipeline_mode=pl.Buffered(1))


def _proj_kernel(x_ref, anw_ref, win_ref, qnw_ref, wuq_ref, kvnw_ref, wukv_ref, qhw_ref, khw_ref,
                 dqw_ref, dkw_ref, bg_ref, rc_ref, rs_ref,
                 q_out, k_out, v_out, qd_out, kd_out, vd_out, ga_out, gb_out):
    x = x_ref[...]
    hb = _rms(x, anw_ref[...]).astype(MXU_DTYPE)

    def seg(ab):
        return jnp.dot(hb, win_ref[:, ab[0]:ab[1]], preferred_element_type=F32)

    rc, rs = rc_ref[...], rs_ref[...]
    qhw, khw = qhw_ref[...], khw_ref[...]
    cos_q, sin_q = rc * qhw[0:1], rs * qhw[1:2]
    cos_k, sin_k = rc * khw[0:1], rs * khw[1:2]

    def head_norm(v):
        return lax.rsqrt(jnp.sum(v * v, axis=-1, keepdims=True) * (1.0 / MLA_QK_DIM) + EPS)

    cqn = _rms(seg(_SEG_CQ), qnw_ref[...]).astype(MXU_DTYPE)
    gw = _HEAD_GROUP * 2 * LANES
    for grp in range(MLA_HEADS // _HEAD_GROUP):
        blk = jnp.dot(cqn, wuq_ref[:, grp * gw:(grp + 1) * gw], preferred_element_type=F32)
        for hh in range(_HEAD_GROUP):
            qh = blk[:, (2 * hh) * LANES:(2 * hh + 1) * LANES]
            qp = blk[:, (2 * hh + 1) * LANES:(2 * hh + 2) * LANES]
            q_out[grp * _HEAD_GROUP + hh] = ((qh * cos_q + qp * sin_q) * head_norm(qh)).astype(q_out.dtype)

    ckvn = _rms(seg(_SEG_CKV), kvnw_ref[...]).astype(MXU_DTYPE)
    kr = seg(_SEG_KR)
    krp = seg(_SEG_KRR) * sin_k
    v_out[...] = jnp.dot(ckvn, wukv_ref[:, MLA_HEADS * LANES:],
                         preferred_element_type=F32).astype(v_out.dtype)
    gw = _HEAD_GROUP * LANES
    for grp in range(MLA_HEADS // _HEAD_GROUP):
        blk = jnp.dot(ckvn, wukv_ref[:, grp * gw:(grp + 1) * gw], preferred_element_type=F32)
        for hh in range(_HEAD_GROUP):
            kh = blk[:, hh * LANES:(hh + 1) * LANES] + kr
            k_out[grp * _HEAD_GROUP + hh] = ((kh * cos_k + krp) * head_norm(kh)).astype(k_out.dtype)

    lo = lax.broadcasted_iota(jnp.int32, (1, LANES), 1) < DIFF_QK_DIM

    def halfnorm(blk, w):
        sq = blk * blk
        ss_lo = jnp.sum(jnp.where(lo, sq, 0.0), axis=-1, keepdims=True)
        ss_hi = jnp.sum(jnp.where(lo, 0.0, sq), axis=-1, keepdims=True)
        r = jnp.where(lo, lax.rsqrt(ss_lo * (1.0 / DIFF_QK_DIM) + EPS),
                      lax.rsqrt(ss_hi * (1.0 / DIFF_QK_DIM) + EPS))
        return blk * r * w

    dq = seg(_SEG_DQ)
    dqw = dqw_ref[...]
    for h in range(DIFF_HEADS):
        qn = halfnorm(dq[:, h * LANES:(h + 1) * LANES], dqw)
        qd_out[h, 0] = jnp.where(lo, qn, 0.0).astype(qd_out.dtype)
        qd_out[h, 1] = jnp.where(lo, 0.0, qn).astype(qd_out.dtype)
    dk = seg(_SEG_DK)
    dkw = dkw_ref[...]
    for h in range(DIFF_HEADS):
        kd_out[h] = halfnorm(dk[:, h * LANES:(h + 1) * LANES], dkw).astype(kd_out.dtype)
    vd_out[...] = seg(_SEG_DV).astype(vd_out.dtype)

    bg = bg_ref[...]
    ga_out[...] = _sigmoid(seg(_SEG_GA) + bg[:, :D_MODEL]).astype(ga_out.dtype)
    gb_out[...] = _sigmoid(seg(_SEG_GB) + bg[:, D_MODEL:]).astype(gb_out.dtype)


def _proj_call(x2d, b, t, anw, win, qnw, wuq, kvnw, wukv, qhw, khw, dqw, dkw, bg, rc, rs, tm):
    n = b * t
    tpb = t // tm
    row = lambda i: (i, 0)
    head = lambda i: (i // tpb, 0, i % tpb, 0)
    act = lambda: jax.ShapeDtypeStruct((n, D_MODEL), MXU_DTYPE)
    hd = lambda nh: jax.ShapeDtypeStruct((b, nh, t, LANES), MXU_DTYPE)
    return pl.pallas_call(
        _proj_kernel,
        grid=(n // tm,),
        in_specs=[
            pl.BlockSpec((tm, D_MODEL), row),
            _const_spec((1, D_MODEL)),
            _const_spec((D_MODEL, _IN_WIDTH_PACKED)),
            _const_spec((1, MLA_Q_RANK)),
            _const_spec((MLA_Q_RANK, MLA_HEADS * 2 * LANES)),
            _const_spec((1, MLA_KV_RANK)),
            _const_spec((MLA_KV_RANK, MLA_HEADS * LANES + D_MODEL)),
            _const_spec((2, LANES)), _const_spec((2, LANES)),
            _const_spec((1, LANES)), _const_spec((1, LANES)),
            _const_spec((1, 2 * D_MODEL)),
            pl.BlockSpec((tm, LANES), row), pl.BlockSpec((tm, LANES), row),
        ],
        out_specs=[
            pl.BlockSpec((None, MLA_HEADS, tm, LANES), head),
            pl.BlockSpec((None, MLA_HEADS, tm, LANES), head),
            pl.BlockSpec((tm, D_MODEL), row),
            pl.BlockSpec((None, DIFF_HEADS, 2, tm, LANES), lambda i: (i // tpb, 0, 0, i % tpb, 0)),
            pl.BlockSpec((None, DIFF_HEADS, tm, LANES), head),
            pl.BlockSpec((tm, D_MODEL), row),
            pl.BlockSpec((tm, D_MODEL), row),
            pl.BlockSpec((tm, D_MODEL), row),
        ],
        out_shape=[hd(MLA_HEADS), hd(MLA_HEADS), act(),
                   jax.ShapeDtypeStruct((b, DIFF_HEADS, 2, t, LANES), MXU_DTYPE), hd(DIFF_HEADS),
                   act(), act(), act()],
        compiler_params=pltpu.CompilerParams(dimension_semantics=("arbitrary",),
                                             vmem_limit_bytes=VMEM_LIMIT),
        name="proj",
    )(x2d, anw, win, qnw, wuq, kvnw, wukv, qhw, khw, dqw, dkw, bg, rc, rs)


_NT = (((1,), (1,)), ((), ()))


def _attn_step(g, q, k_ref, v_ref, s_ref, mx_ref, ls_ref, acc_ref, tk, penalty_fn=None):
    tq = q.shape[0]
    nk = k_ref.shape[0] // tk

    @pl.when(g == 0)
    def _():
        s_ref[1][...] = jnp.zeros_like(s_ref[1])
        mx_ref[1][...] = jnp.zeros_like(mx_ref[1])

    ls_ref[...] = jnp.zeros_like(ls_ref)
    acc_ref[...] = jnp.zeros_like(acc_ref)

    def run(s_cur, s_prv, mx_cur, mx_prv):
        m_prev = jnp.max(mx_prv[...], axis=-1, keepdims=True)
        mx_cur[...] = jnp.full((tq, LANES), -jnp.inf, F32)

        def body(kc, _):
            off = pl.multiple_of(kc * tk, tk)
            s = lax.dot_general(q, k_ref[pl.ds(off, tk), :], _NT, preferred_element_type=F32)
            if penalty_fn is not None:
                s = s - penalty_fn(off)
            s_cur[:, pl.ds(off, tk)] = s
            mx_cur[...] = jnp.maximum(mx_cur[...], _fold_lanes(s, jnp.maximum))

            p = jnp.exp2(s_prv[:, pl.ds(off, tk)] - m_prev)
            ls_ref[...] += _fold_lanes(p, jnp.add)
            acc_ref[...] += jnp.dot(p.astype(MXU_DTYPE), v_ref[pl.ds(off, tk), :],
                                    preferred_element_type=F32)
            return 0

        lax.fori_loop(0, nk, body, 0, unroll=_pick(nk, (4, 2)))

    for par in range(2):
        @pl.when(g % 2 == par)
        def _():
            run(s_ref[par], s_ref[1 - par], mx_ref[par], mx_ref[1 - par])

    return acc_ref[...] / jnp.sum(ls_ref[...], axis=-1, keepdims=True)


def _attn_scratch(tq, t):
    return [[pltpu.VMEM((tq, t), F32)] * 2, [pltpu.VMEM((tq, LANES), F32)] * 2,
            pltpu.VMEM((tq, LANES), F32), pltpu.VMEM((tq, LANES), F32)]


def _mla_kernel(q_ref, k_ref, v_ref, o_ref, s_ref, mx_ref, ls_ref, acc_ref, *, tk):
    g = pl.program_id(0)
    o = _attn_step(g, q_ref[...], k_ref, v_ref, s_ref, mx_ref, ls_ref, acc_ref, tk)
    for half in range(2):
        @pl.when((g > 0) & ((g - 1) % 2 == half))
        def _():
            sl = slice(half * MLA_V_DIM, (half + 1) * MLA_V_DIM)
            o_ref[:, sl] = o[:, sl].astype(o_ref.dtype)


def _mla_call(q, k, v, tq, tk):
    b, nh, t, _ = q.shape
    nq = t // tq
    units = b * nh * nq

    def unit(u):
        return u // (nh * nq), (u // (2 * nq)) % (nh // 2), (u // 2) % nq, u % 2

    def cur(g):
        return unit(jnp.minimum(g, units - 1))

    def prev(g):
        return unit(jnp.maximum(g - 1, 0))

    def q_map(g):
        bi, p, i, h = cur(g)
        return bi, 2 * p + h, i, 0

    def k_map(g):
        bi, p, _, h = cur(g)
        return bi, 2 * p + h, 0, 0

    def v_map(g):
        bi, p, _, _ = prev(g)
        return bi, 0, p

    def o_map(g):
        bi, p, i, _ = prev(g)
        return bi, i, p

    return pl.pallas_call(
        functools.partial(_mla_kernel, tk=tk),
        grid=(units + 1,),
        in_specs=[
            pl.BlockSpec((None, None, tq, LANES), q_map),
            pl.BlockSpec((None, None, t, LANES), k_map),
            pl.BlockSpec((None, t, LANES), v_map),
        ],
        out_specs=pl.BlockSpec((None, tq, LANES), o_map),
        out_shape=jax.ShapeDtypeStruct((b, t, D_MODEL), MXU_DTYPE),
        scratch_shapes=_attn_scratch(tq, t),
        compiler_params=pltpu.CompilerParams(dimension_semantics=("arbitrary",),
                                             vmem_limit_bytes=VMEM_LIMIT),
        name="mla_attn",
    )(q, k, v)


def _diff_kernel(sc_ref, q_ref, k_ref, v_ref, pq_ref, pk_ref, w_ref, o_ref,
                 s_ref, mx_ref, ls_ref, acc_ref, o1_ref, *, tk, nq, units, out_scale):
    g = pl.program_id(0)
    h_cur = (jnp.minimum(g, units - 1) // (2 * nq)) % DIFF_HEADS
    slope = sc_ref[h_cur]
    lam = sc_ref[DIFF_HEADS]
    pq = pq_ref[...] * slope

    def penalty(off):
        return jnp.abs(pq - pk_ref[:, pl.ds(off, tk)] * slope)

    o = _attn_step(g, q_ref[...], k_ref, v_ref, s_ref, mx_ref, ls_ref, acc_ref, tk, penalty)

    @pl.when((g - 1) % 2 == 0)
    def _():
        o1_ref[...] = o

    @pl.when((g > 0) & ((g - 1) % 2 == 1))
    def _():
        od = o1_ref[...] - lam * o
        o_ref[...] = (_rms(od, w_ref[...]) * out_scale).astype(o_ref.dtype)


def _diff_call(scalars, qd, k, v, posq, posk, subw, tq, tk, out_scale):
    b, nh, _, t, _ = qd.shape
    nq = t // tq
    units = b * nh * nq * 2

    def unit(u):
        return u // (2 * nh * nq), (u // (2 * nq)) % nh, (u // 2) % nq, u % 2

    def cur(g):
        return unit(jnp.minimum(g, units - 1))

    def prev(g):
        return unit(jnp.maximum(g - 1, 0))

    def q_map(g):
        bi, h, i, m = cur(g)
        return bi, h, m, i, 0

    def k_map(g):
        bi, h, _, _ = cur(g)
        return bi, h, 0, 0

    def v_map(g):
        bi, h, _, _ = prev(g)
        return bi, 0, h

    def o_map(g):
        bi, h, i, _ = prev(g)
        return bi, i, h

    return pl.pallas_call(
        functools.partial(_diff_kernel, tk=tk, nq=nq, units=units, out_scale=out_scale),
        grid=(units + 1,),
        in_specs=[
            pl.BlockSpec(memory_space=pltpu.SMEM),
            pl.BlockSpec((None, None, None, tq, LANES), q_map),
            pl.BlockSpec((None, None, t, LANES), k_map),
            pl.BlockSpec((None, t, LANES), v_map),
            pl.BlockSpec((None, tq, 1), lambda g: (cur(g)[0], cur(g)[2], 0)),
            pl.BlockSpec((None, 1, t), lambda g: (cur(g)[0], 0, 0)),
            pl.BlockSpec((1, LANES), lambda g: (0, 0)),
        ],
        out_specs=pl.BlockSpec((None, tq, LANES), o_map),
        out_shape=jax.ShapeDtypeStruct((b, t, D_MODEL), MXU_DTYPE),
        scratch_shapes=_attn_scratch(tq, t) + [pltpu.VMEM((tq, LANES), F32)],
        compiler_params=pltpu.CompilerParams(dimension_semantics=("arbitrary",),
                                             vmem_limit_bytes=VMEM_LIMIT),
        name="diff_attn",
    )(scalars, qd, k, v, posq, posk, subw)


def _split2(a):
    a1 = a.astype(jnp.bfloat16)
    return a1, (a - a1.astype(F32)).astype(jnp.bfloat16)


def _merge_kernel(x_ref, om_ref, od_ref, ga_ref, gb_ref, wo_ref, fnw_ref, rw1_ref, rw2_ref,
                  x1_out, aff_out, h2_out):
    merged = (ga_ref[...].astype(F32) * om_ref[...].astype(F32)
              + gb_ref[...].astype(F32) * od_ref[...].astype(F32))
    x1 = x_ref[...] + jnp.dot(merged.astype(MXU_DTYPE), wo_ref[...], preferred_element_type=F32)
    x1_out[...] = x1
    h2 = _rms(x1, fnw_ref[...])

    h2_out[...] = h2

    h1, hh2 = _split2(h2)
    w1, w2 = rw1_ref[...], rw2_ref[...]
    dot = lambda a, w: jnp.dot(a, w, preferred_element_type=F32)
    logits = (dot(hh2, w1) + dot(h1, w2)) + dot(h1, w1)
    lane = lax.broadcasted_iota(jnp.int32, (1, LANES), 1)
    logits = jnp.where(lane < N_EXPERTS, logits, -jnp.inf)
    e = jnp.exp(logits - jnp.max(logits, axis=-1, keepdims=True))
    aff = e / jnp.sum(e, axis=-1, keepdims=True)
    aff_out[...] = aff[:, :N_EXPERTS]


def _merge_call(x2d, om, od, ga, gb, wo, fnw, rw1, rw2, tm):
    n = x2d.shape[0]
    row = lambda i: (i, 0)
    act = pl.BlockSpec((tm, D_MODEL), row)
    return pl.pallas_call(
        _merge_kernel,
        grid=(n // tm,),
        in_specs=[act, act, act, act, act,
                  _const_spec((D_MODEL, D_MODEL)), _const_spec((1, D_MODEL)),
                  _const_spec((D_MODEL, LANES)), _const_spec((D_MODEL, LANES))],
        out_specs=[act, pl.BlockSpec((tm, N_EXPERTS), row), act],
        out_shape=[jax.ShapeDtypeStruct((n, D_MODEL), F32),
                   jax.ShapeDtypeStruct((n, N_EXPERTS), F32),
                   jax.ShapeDtypeStruct((n, D_MODEL), F32)],
        compiler_params=pltpu.CompilerParams(dimension_semantics=("arbitrary",),
                                             vmem_limit_bytes=VMEM_LIMIT),
        name="merge",
    )(x2d, om, od, ga, gb, wo, fnw, rw1, rw2)


def _route_kernel(a_ref, tri_ref, idx_out, gate_out, rank_ref, *, cap, jc):
    a = a_ref[...]
    ne, t = a.shape

    def search(_, lohi):
        lo, hi = lohi
        mid = lo + ((hi - lo + 1) >> 1)
        cnt = jnp.sum(jnp.where(a >= pltpu.bitcast(mid, F32), 1.0, 0.0), axis=-1, keepdims=True)
        ok = cnt >= cap
        return jnp.where(ok, mid, lo), jnp.where(ok, hi, mid - 1)

    lo0 = jnp.zeros((ne, 1), jnp.int32)
    hi0 = jnp.full((ne, 1), 0x7F800000, jnp.int32)
    thr_bits, _ = lax.fori_loop(0, 32, search, (lo0, hi0))
    thr = pltpu.bitcast(thr_bits, F32)

    tri = tri_ref[...]

    def prefix(mask):
        carry = jnp.zeros((ne, 1), F32)
        cols = []
        for c in range(t // LANES):
            m = mask[:, c * LANES:(c + 1) * LANES]
            cols.append(jnp.dot(m.astype(jnp.bfloat16), tri, preferred_element_type=F32) + carry)
            carry = carry + jnp.sum(m, axis=-1, keepdims=True)
        return jnp.concatenate(cols, axis=1), carry

    gt = jnp.where(a > thr, 1.0, 0.0)
    eq = jnp.where(a == thr, 1.0, 0.0)
    pre_eq, _ = prefix(eq)
    need = cap - jnp.sum(gt, axis=-1, keepdims=True)
    sel = jnp.maximum(gt, jnp.where(pre_eq < need, eq, 0.0))
    rank, _ = prefix(sel)
    rank_ref[...] = jnp.where(sel > 0.0, rank, -1.0)

    tok = lax.broadcasted_iota(jnp.int32, (1, t), 1).astype(F32)

    def per_expert(e, _):
        r = rank_ref[pl.ds(e, 1), :]
        av = a_ref[pl.ds(e, 1), :]

        def per_slots(c, _):
            j0 = pl.multiple_of(c * jc, jc)
            slot = (lax.broadcasted_iota(jnp.int32, (jc, 1), 0) + j0).astype(F32)
            hit = r == slot
            idx_out[e, pl.ds(j0, jc), :] = jnp.sum(jnp.where(hit, tok, 0.0), axis=-1,
                                                   keepdims=True).astype(jnp.int32)
            gate_out[e, pl.ds(j0, jc), :] = jnp.sum(jnp.where(hit, av, 0.0), axis=-1, keepdims=True)
            return 0

        return lax.fori_loop(0, cap // jc, per_slots, 0)

    lax.fori_loop(0, ne, per_expert, 0)


def _route_call(aff_t, tri, cap, jc):
    b, ne, t = aff_t.shape
    return pl.pallas_call(
        functools.partial(_route_kernel, cap=cap, jc=jc),
        grid=(b,),
        in_specs=[pl.BlockSpec((None, ne, t), lambda bi: (bi, 0, 0)), _const_spec((LANES, LANES))],
        out_specs=[pl.BlockSpec((None, ne, cap, 1), lambda bi: (bi, 0, 0, 0)),
                   pl.BlockSpec((None, ne, cap, 1), lambda bi: (bi, 0, 0, 0))],
        out_shape=[jax.ShapeDtypeStruct((b, ne, cap, 1), jnp.int32),
                   jax.ShapeDtypeStruct((b, ne, cap, 1), F32)],
        scratch_shapes=[pltpu.VMEM((ne, t), F32)],
        compiler_params=pltpu.CompilerParams(dimension_semantics=("arbitrary",),
                                             vmem_limit_bytes=VMEM_LIMIT),
        name="route",
    )(aff_t, tri)


_ROW_UNROLL = 8


_FF_CHUNK = 512


def _expert_kernel(idx_ref, idxn_ref, h2_hbm, gate_ref, wg_ref, wu_ref, wd_ref, y_out,
                   xe_ref, hid_ref, sem, *, cap):
    nb = pl.num_programs(1)
    step = pl.program_id(0) * nb + pl.program_id(1)
    last = pl.num_programs(0) * nb - 1
    cur = step % 2
    nxt = 1 - cur
    b_next = jnp.minimum(step + 1, last) % nb

    def row_copy(table_ref, b_src, slot, j):
        return pltpu.make_async_copy(h2_hbm.at[b_src, pl.ds(table_ref[0, j], 1), :],
                                     xe_ref.at[slot, pl.ds(j, 1), :], sem.at[slot])

    def wait_slot(slot):
        pltpu.make_async_copy(xe_ref.at[slot], xe_ref.at[slot], sem.at[slot]).wait()

    @pl.when(step == 0)
    def _():
        def first(j, _):
            row_copy(idx_ref, pl.program_id(1), 0, j).start()
            return 0

        lax.fori_loop(0, cap, first, 0)

    for j in range(cap):
        row_copy(idxn_ref, b_next, nxt, j).start()

    wait_slot(cur)
    xe = xe_ref[cur].astype(MXU_DTYPE)
    for c in range(EXPERT_FF // _FF_CHUNK):
        cols = slice(c * _FF_CHUNK, (c + 1) * _FF_CHUNK)
        g = jnp.dot(xe, wg_ref[:, cols], preferred_element_type=F32)
        u = jnp.dot(xe, wu_ref[:, cols], preferred_element_type=F32)
        hid_ref[:, cols] = (g * _sigmoid(g) * u).astype(MXU_DTYPE)
    y = jnp.dot(hid_ref[...], wd_ref[...], preferred_element_type=F32)
    y_out[...] = y * gate_ref[...]

    @pl.when(step == last)
    def _():
        wait_slot(nxt)


def _expert_call(idx_smem, h2, gate, wg, wu, wd, cap):
    b, t, d = h2.shape
    ne = wg.shape[0]

    def next_idx(e, bi):
        s = jnp.minimum(e * b + bi + 1, ne * b - 1)
        return (s % b) * ne + s // b, 0, 0

    return pl.pallas_call(
        functools.partial(_expert_kernel, cap=cap),
        grid=(ne, b),
        in_specs=[
            pl.BlockSpec((None, 1, cap), lambda e, bi: (bi * ne + e, 0, 0), memory_space=pltpu.SMEM),
            pl.BlockSpec((None, 1, cap), next_idx, memory_space=pltpu.SMEM),
            pl.BlockSpec(memory_space=pl.ANY),
            pl.BlockSpec((None, None, cap, 1), lambda e, bi: (bi, e, 0, 0)),
            pl.BlockSpec((None, D_MODEL, EXPERT_FF), lambda e, bi: (e, 0, 0)),
            pl.BlockSpec((None, D_MODEL, EXPERT_FF), lambda e, bi: (e, 0, 0)),
            pl.BlockSpec((None, EXPERT_FF, D_MODEL), lambda e, bi: (e, 0, 0)),
        ],
        out_specs=pl.BlockSpec((None, None, cap, D_MODEL), lambda e, bi: (bi, e, 0, 0)),
        out_shape=jax.ShapeDtypeStruct((b, ne, cap, D_MODEL), F32),
        scratch_shapes=[pltpu.VMEM((2, cap, d), F32), pltpu.VMEM((cap, EXPERT_FF), MXU_DTYPE),
                        pltpu.SemaphoreType.DMA((2,))],
        compiler_params=pltpu.CompilerParams(dimension_semantics=("arbitrary",) * 2,
                                             vmem_limit_bytes=VMEM_LIMIT),
        name="experts",
    )(idx_smem, idx_smem, h2, gate, wg, wu, wd)


def _combine_kernel(idx_ref, x1_hbm, y_ref, o_hbm, acc_ref, sem, *, cap):
    bi, e = pl.program_id(0), pl.program_id(1)

    @pl.when(e == 0)
    def _():
        load = pltpu.make_async_copy(x1_hbm.at[bi], acc_ref, sem)
        load.start()
        load.wait()

    def scatter(g, _):
        toks = [idx_ref[0, g * _ROW_UNROLL + u] for u in range(_ROW_UNROLL)]
        rows = [acc_ref[pl.ds(toks[u], 1), :] + y_ref[pl.ds(g * _ROW_UNROLL + u, 1), :]
                for u in range(_ROW_UNROLL)]
        for u in range(_ROW_UNROLL):
            acc_ref[pl.ds(toks[u], 1), :] = rows[u]
        return 0

    lax.fori_loop(0, cap // _ROW_UNROLL, scatter, 0)

    @pl.when(e == pl.num_programs(1) - 1)
    def _():
        store = pltpu.make_async_copy(acc_ref, o_hbm.at[bi], sem)
        store.start()
        store.wait()


def _combine_call(idx_smem, x1, y, cap):
    b, t, d = x1.shape
    ne = y.shape[1]
    return pl.pallas_call(
        functools.partial(_combine_kernel, cap=cap),
        grid=(b, ne),
        in_specs=[
            pl.BlockSpec((None, 1, cap), lambda bi, e: (bi * ne + e, 0, 0), memory_space=pltpu.SMEM),
            pl.BlockSpec(memory_space=pl.ANY),
            pl.BlockSpec((None, None, cap, d), lambda bi, e: (bi, e, 0, 0)),
        ],
        out_specs=pl.BlockSpec(memory_space=pl.ANY),
        out_shape=jax.ShapeDtypeStruct((b, t, d), F32),
        scratch_shapes=[pltpu.VMEM((t, d), F32), pltpu.SemaphoreType.DMA(())],
        compiler_params=pltpu.CompilerParams(dimension_semantics=("arbitrary",) * 2,
                                             vmem_limit_bytes=VMEM_LIMIT),
        name="combine",
    )(idx_smem, x1, y)


def _pad_heads(w, n_heads, width):
    r = w.shape[0]
    w = w.reshape(r, n_heads, width)
    return jnp.pad(w, ((0, 0), (0, 0), (0, LANES - width))).reshape(r, n_heads * LANES)


def _pack_w_in(w_in):
    o = 0
    parts = {}
    for name, width in (("cq", MLA_Q_RANK), ("ckv", MLA_KV_RANK), ("kr", MLA_ROPE_DIM),
                        ("dq", D_MODEL), ("dk", D_MODEL), ("dv", D_MODEL), ("ga", D_MODEL), ("gb", D_MODEL)):
        parts[name] = w_in[:, o:o + width]
        o += width
    kr = jnp.pad(parts["kr"], ((0, 0), (MLA_NOPE_DIM, LANES - MLA_QK_DIM)))
    return jnp.concatenate([parts["cq"], parts["ckv"], kr, _rope_partner(kr), parts["dq"], parts["dk"],
                            parts["dv"], parts["ga"], parts["gb"]], axis=1).astype(MXU_DTYPE)


def _rope_partner(w):
    half = MLA_ROPE_DIM // 2
    lane = jnp.arange(LANES)
    src = jnp.where((lane >= MLA_NOPE_DIM) & (lane < MLA_NOPE_DIM + half), lane + half,
                    jnp.where((lane >= MLA_NOPE_DIM + half) & (lane < MLA_QK_DIM), lane - half, lane))
    blocks = w.reshape(w.shape[:-1] + (w.shape[-1] // LANES, LANES))
    return jnp.take(blocks, src, axis=-1).reshape(w.shape)


def _rope_tables(positions):
    inv_freq = 1.0 / (ROPE_THETA ** (jnp.arange(0, MLA_ROPE_DIM, 2, dtype=F32) / MLA_ROPE_DIM))
    ang = positions.astype(F32).reshape(-1, 1) * inv_freq
    cos, sin = jnp.cos(ang), jnp.sin(ang)
    n = ang.shape[0]
    z = lambda w: jnp.zeros((n, w), F32)
    tail = LANES - MLA_QK_DIM
    rc = jnp.concatenate([jnp.ones((n, MLA_NOPE_DIM), F32), cos, cos, z(tail)], axis=1)
    rs = jnp.concatenate([z(MLA_NOPE_DIM), -sin, sin, z(tail)], axis=1)
    return rc, rs


def _pick(t, prefs):
    for p in prefs:
        if t % p == 0:
            return p
    return t


def _layer(x, positions, layer_idx, attn_norm_w, w_in, b_gate, mla_q_norm_w, mla_w_uq, mla_kv_norm_w,
           mla_w_ukv, mla_q_hnorm_w, mla_k_hnorm_w, diff_q_hnorm_w, diff_k_hnorm_w, diff_lambda,
           diff_subln_w, w_out, ffn_norm_w, router_w, expert_w_gate, expert_w_up, expert_w_down):
    b, t, d = x.shape
    n = b * t
    cap = CAPACITY_FACTOR * t // N_EXPERTS
    row = lambda v: v.reshape(1, -1).astype(F32)

    win = _pack_w_in(w_in)
    wuq = _pad_heads(mla_w_uq, MLA_HEADS, MLA_QK_DIM)
    wuq = jnp.stack([wuq.reshape(MLA_Q_RANK, MLA_HEADS, LANES),
                     _rope_partner(wuq).reshape(MLA_Q_RANK, MLA_HEADS, LANES)], axis=2)
    wuq = wuq.reshape(MLA_Q_RANK, MLA_HEADS * 2 * LANES).astype(MXU_DTYPE)
    wukv = mla_w_ukv.reshape(MLA_KV_RANK, MLA_HEADS, MLA_NOPE_DIM + MLA_V_DIM)
    wuk = _pad_heads(wukv[:, :, :MLA_NOPE_DIM].reshape(MLA_KV_RANK, -1), MLA_HEADS, MLA_NOPE_DIM)
    wuv = wukv[:, :, MLA_NOPE_DIM:].reshape(MLA_KV_RANK, -1)
    wukv_p = jnp.concatenate([wuk, wuv], axis=1).astype(MXU_DTYPE)
    pad_gain = lambda w: jnp.pad(w.astype(F32), (0, LANES - MLA_QK_DIM)).reshape(1, LANES)
    qhw = pad_gain(mla_q_hnorm_w) * (MLA_QK_DIM ** -0.5 * LOG2E)
    khw = pad_gain(mla_k_hnorm_w)
    qhw = jnp.concatenate([qhw, _rope_partner(qhw)], axis=0)
    khw = jnp.concatenate([khw, _rope_partner(khw)], axis=0)
    dqw = jnp.tile(diff_q_hnorm_w.astype(F32), 2).reshape(1, LANES) * (DIFF_QK_DIM ** -0.5 * LOG2E)
    dkw = jnp.tile(diff_k_hnorm_w.astype(F32), 2).reshape(1, LANES)
    rc, rs = _rope_tables(positions)

    tm = _pick(t, (512, 256, 128))
    q, k, v, qd, kd, vd, ga, gb = _proj_call(
        x.reshape(n, d), b, t, row(attn_norm_w), win, row(mla_q_norm_w), wuq, row(mla_kv_norm_w), wukv_p,
        qhw, khw, dqw, dkw, row(b_gate), rc, rs, tm)

    tq = _pick(t, (1024, 512, 256, 128))
    tk = _pick(t, (512, 256, 128))
    o_mla = _mla_call(q, k, v.reshape(b, t, d), tq, tk)

    lam_init = 0.8 - 0.6 * math.exp(-0.3 * layer_idx)
    lf = diff_lambda.astype(F32)
    lam = jnp.exp(jnp.sum(lf[0] * lf[1])) - jnp.exp(jnp.sum(lf[2] * lf[3])) + lam_init
    slopes = jnp.asarray([2.0 ** (-8.0 * (i + 1) / DIFF_HEADS) * LOG2E for i in range(DIFF_HEADS)], F32)
    scalars = jnp.concatenate([slopes, lam.reshape(1)])
    posf = positions.astype(F32)
    o_diff = _diff_call(scalars, qd, kd, vd.reshape(b, t, d), posf.reshape(b, t, 1),
                        posf.reshape(b, 1, t), row(diff_subln_w), tq, tk, 1.0 - lam_init)

    rw = jnp.pad(router_w.astype(F32), ((0, 0), (0, LANES - N_EXPERTS)))
    rw1, rw2 = _split2(rw)
    tm2 = _pick(t, (512, 256, 128))
    x1, aff, h2 = _merge_call(x.reshape(n, d), o_mla.reshape(n, d), o_diff.reshape(n, d), ga, gb,
                              w_out.astype(MXU_DTYPE), row(ffn_norm_w), rw1, rw2, tm2)

    tri = (lax.broadcasted_iota(jnp.int32, (LANES, LANES), 0)
           < lax.broadcasted_iota(jnp.int32, (LANES, LANES), 1)).astype(jnp.bfloat16)
    aff_t = jnp.swapaxes(aff.reshape(b, t, N_EXPERTS), 1, 2)
    idx, gate = _route_call(aff_t, tri, cap, _pick(cap, (128, 64, 32, 16, 8)))
    idx_smem = idx.reshape(b * N_EXPERTS, 1, cap)

    y = _expert_call(idx_smem, h2.reshape(b, t, d), gate, expert_w_gate.astype(MXU_DTYPE),
                     expert_w_up.astype(MXU_DTYPE), expert_w_down.astype(MXU_DTYPE), cap)
    return _combine_call(idx_smem, x1.reshape(b, t, d), y, cap)


def kernel(x, positions, attn_norm_w, w_in, b_gate, mla_q_norm_w, mla_w_uq, mla_kv_norm_w, mla_w_ukv,
           mla_q_hnorm_w, mla_k_hnorm_w, diff_q_hnorm_w, diff_k_hnorm_w, diff_lambda, diff_subln_w, w_out,
           ffn_norm_w, router_w, expert_w_gate, expert_w_up, expert_w_down):
    for l in range(attn_norm_w.shape[0]):
        x = _layer(x, positions, l, attn_norm_w[l], w_in[l], b_gate[l], mla_q_norm_w[l], mla_w_uq[l],
                   mla_kv_norm_w[l], mla_w_ukv[l], mla_q_hnorm_w[l], mla_k_hnorm_w[l], diff_q_hnorm_w[l],
                   diff_k_hnorm_w[l], diff_lambda[l], diff_subln_w[l], w_out[l], ffn_norm_w[l], router_w[l],
                   expert_w_gate[l], expert_w_up[l], expert_w_down[l])
    return x
```

```python
import functools
import math

import jax
import jax.numpy as jnp
from jax import lax
from jax.experimental import pallas as pl
from jax.experimental.pallas import tpu as pltpu

F32 = jnp.float32
MXU_DTYPE = jnp.bfloat16

D_MODEL = 1024
MLA_HEADS = 16
MLA_Q_RANK = 256
MLA_KV_RANK = 256
MLA_NOPE_DIM = 64
MLA_ROPE_DIM = 32
MLA_QK_DIM = MLA_NOPE_DIM + MLA_ROPE_DIM
MLA_V_DIM = D_MODEL // MLA_HEADS
ROPE_THETA = 10000.0
DIFF_HEADS = 8
DIFF_QK_DIM = 64
DIFF_V_DIM = D_MODEL // DIFF_HEADS
N_EXPERTS = 16
CAPACITY_FACTOR = 2
EXPERT_FF = 2 * D_MODEL
EPS = 1e-6
LOG2E = math.log2(math.e)

LANES = 128
SUBLANES = 8
VMEM_LIMIT = 56 * 1024 * 1024

_SEG_CQ = (0, 256)
_SEG_CKV = (256, 512)
_SEG_KR = (512, 640)
_SEG_KRR = (640, 768)
_SEG_DQ = (768, 1792)
_SEG_DK = (1792, 2816)
_SEG_DV = (2816, 3840)
_SEG_GA = (3840, 4864)
_SEG_GB = (4864, 5888)
_IN_WIDTH_PACKED = 5888
_HEAD_GROUP = 4


def _rms(x, w):
    return x * lax.rsqrt(jnp.mean(x * x, axis=-1, keepdims=True) + EPS) * w


def _sigmoid(x):
    return 1.0 / (1.0 + jnp.exp(-x))


def _fold_lanes(x, op):
    parts = [x[:, i * LANES:(i + 1) * LANES] for i in range(x.shape[1] // LANES)]
    return functools.reduce(op, parts)


def _const_spec(shape):
    return pl.BlockSpec(shape, lambda *_: (0,) * len(shape), pipeline_mode=pl.Buffered(1))


def _proj_kernel(x_ref, anw_ref, win_ref, qnw_ref, wuq_ref, kvnw_ref, wukv_ref, qhw_ref, khw_ref,
                 dqw_ref, dkw_ref, bg_ref, rc_ref, rs_ref,
                 q_out, k_out, v_out, qd_out, kd_out, vd_out, ga_out, gb_out):
    x = x_ref[...]
    hb = _rms(x, anw_ref[...]).astype(MXU_DTYPE)

    def seg(ab):
        return jnp.dot(hb, win_ref[:, ab[0]:ab[1]], preferred_element_type=F32)

    rc, rs = rc_ref[...], rs_ref[...]
    qhw, khw = qhw_ref[...], khw_ref[...]
    cos_q, sin_q = rc * qhw[0:1], rs * qhw[1:2]
    cos_k, sin_k = rc * khw[0:1], rs * khw[1:2]

    def head_norm(v):
        return lax.rsqrt(jnp.sum(v * v, axis=-1, keepdims=True) * (1.0 / MLA_QK_DIM) + EPS)

    cqn = _rms(seg(_SEG_CQ), qnw_ref[...]).astype(MXU_DTYPE)
    gw = _HEAD_GROUP * 2 * LANES
    for grp in range(MLA_HEADS // _HEAD_GROUP):
        blk = jnp.dot(cqn, wuq_ref[:, grp * gw:(grp + 1) * gw], preferred_element_type=F32)
        for hh in range(_HEAD_GROUP):
            qh = blk[:, (2 * hh) * LANES:(2 * hh + 1) * LANES]
            qp = blk[:, (2 * hh + 1) * LANES:(2 * hh + 2) * LANES]
            q_out[grp * _HEAD_GROUP + hh] = ((qh * cos_q + qp * sin_q) * head_norm(qh)).astype(q_out.dtype)

    ckvn = _rms(seg(_SEG_CKV), kvnw_ref[...]).astype(MXU_DTYPE)
    kr = seg(_SEG_KR)
    krp = seg(_SEG_KRR) * sin_k
    v_out[...] = jnp.dot(ckvn, wukv_ref[:, MLA_HEADS * LANES:],
                         preferred_element_type=F32).astype(v_out.dtype)
    gw = _HEAD_GROUP * LANES
    for grp in range(MLA_HEADS // _HEAD_GROUP):
        blk = jnp.dot(ckvn, wukv_ref[:, grp * gw:(grp + 1) * gw], preferred_element_type=F32)
        for hh in range(_HEAD_GROUP):
            kh = blk[:, hh * LANES:(hh + 1) * LANES] + kr
            k_out[grp * _HEAD_GROUP + hh] = ((kh * cos_k + krp) * head_norm(kh)).astype(k_out.dtype)

    lo = lax.broadcasted_iota(jnp.int32, (1, LANES), 1) < DIFF_QK_DIM

    def halfnorm(blk, w):
        sq = blk * blk
        ss_lo = jnp.sum(jnp.where(lo, sq, 0.0), axis=-1, keepdims=True)
        ss_hi = jnp.sum(jnp.where(lo, 0.0, sq), axis=-1, keepdims=True)
        r = jnp.where(lo, lax.rsqrt(ss_lo * (1.0 / DIFF_QK_DIM) + EPS),
                      lax.rsqrt(ss_hi * (1.0 / DIFF_QK_DIM) + EPS))
        return blk * r * w

    dq = seg(_SEG_DQ)
    dqw = dqw_ref[...]
    for h in range(DIFF_HEADS):
        qn = halfnorm(dq[:, h * LANES:(h + 1) * LANES], dqw)
        qd_out[h, 0] = jnp.where(lo, qn, 0.0).astype(qd_out.dtype)
        qd_out[h, 1] = jnp.where(lo, 0.0, qn).astype(qd_out.dtype)
    dk = seg(_SEG_DK)
    dkw = dkw_ref[...]
    for h in range(DIFF_HEADS):
        kd_out[h] = halfnorm(dk[:, h * LANES:(h + 1) * LANES], dkw).astype(kd_out.dtype)
    vd_out[...] = seg(_SEG_DV).astype(vd_out.dtype)

    bg = bg_ref[...]
    ga_out[...] = _sigmoid(seg(_SEG_GA) + bg[:, :D_MODEL]).astype(ga_out.dtype)
    gb_out[...] = _sigmoid(seg(_SEG_GB) + bg[:, D_MODEL:]).astype(gb_out.dtype)


def _proj_call(x2d, b, t, anw, win, qnw, wuq, kvnw, wukv, qhw, khw, dqw, dkw, bg, rc, rs, tm):
    n = b * t
    tpb = t // tm
    row = lambda i: (i, 0)
    head = lambda i: (i // tpb, 0, i % tpb, 0)
    act = lambda: jax.ShapeDtypeStruct((n, D_MODEL), MXU_DTYPE)
    hd = lambda nh: jax.ShapeDtypeStruct((b, nh, t, LANES), MXU_DTYPE)
    return pl.pallas_call(
        _proj_kernel,
        grid=(n // tm,),
        in_specs=[
            pl.BlockSpec((tm, D_MODEL), row),
            _const_spec((1, D_MODEL)),
            _const_spec((D_MODEL, _IN_WIDTH_PACKED)),
            _const_spec((1, MLA_Q_RANK)),
            _const_spec((MLA_Q_RANK, MLA_HEADS * 2 * LANES)),
            _const_spec((1, MLA_KV_RANK)),
            _const_spec((MLA_KV_RANK, MLA_HEADS * LANES + D_MODEL)),
            _const_spec((2, LANES)), _const_spec((2, LANES)),
            _const_spec((1, LANES)), _const_spec((1, LANES)),
            _const_spec((1, 2 * D_MODEL)),
            pl.BlockSpec((tm, LANES), row), pl.BlockSpec((tm, LANES), row),
        ],
        out_specs=[
            pl.BlockSpec((None, MLA_HEADS, tm, LANES), head),
            pl.BlockSpec((None, MLA_HEADS, tm, LANES), head),
            pl.BlockSpec((tm, D_MODEL), row),
            pl.BlockSpec((None, DIFF_HEADS, 2, tm, LANES), lambda i: (i // tpb, 0, 0, i % tpb, 0)),
            pl.BlockSpec((None, DIFF_HEADS, tm, LANES), head),
            pl.BlockSpec((tm, D_MODEL), row),
            pl.BlockSpec((tm, D_MODEL), row),
            pl.BlockSpec((tm, D_MODEL), row),
        ],
        out_shape=[hd(MLA_HEADS), hd(MLA_HEADS), act(),
                   jax.ShapeDtypeStruct((b, DIFF_HEADS, 2, t, LANES), MXU_DTYPE), hd(DIFF_HEADS),
                   act(), act(), act()],
        compiler_params=pltpu.CompilerParams(dimension_semantics=("arbitrary",),
                                             vmem_limit_bytes=VMEM_LIMIT),
        name="proj",
    )(x2d, anw, win, qnw, wuq, kvnw, wukv, qhw, khw, dqw, dkw, bg, rc, rs)


_NT = (((1,), (1,)), ((), ()))


def _attn_step(g, q, k_ref, v_ref, s_ref, mx_ref, ls_ref, acc_ref, tk, finish, penalty_fn=None):
    tq = q.shape[0]
    nk = k_ref.shape[0] // tk
    rows = tq // nk

    @pl.when(g == 0)
    def _():
        s_ref[1][...] = jnp.zeros_like(s_ref[1])
        mx_ref[1][...] = jnp.zeros_like(mx_ref[1])
        acc_ref[0][...] = jnp.zeros_like(acc_ref[0])
        ls_ref[0][...] = jnp.ones_like(ls_ref[0])

    def run(p):
        s_cur, s_prv, mx_cur = s_ref[p], s_ref[1 - p], mx_ref[p]
        ls_prv, acc_prv, ls_old, acc_old = ls_ref[1 - p], acc_ref[1 - p], ls_ref[p], acc_ref[p]
        m_prev = jnp.max(mx_ref[1 - p][...], axis=-1, keepdims=True)
        mx_cur[...] = jnp.full((tq, LANES), -jnp.inf, F32)
        ls_prv[...] = jnp.zeros_like(ls_prv)
        acc_prv[...] = jnp.zeros_like(acc_prv)

        def body(kc, _):
            off = pl.multiple_of(kc * tk, tk)
            s = lax.dot_general(q, k_ref[pl.ds(off, tk), :], _NT, preferred_element_type=F32)
            if penalty_fn is not None:
                s = s - penalty_fn(off)
            s_cur[:, pl.ds(off, tk)] = s
            mx_cur[...] = jnp.maximum(mx_cur[...], _fold_lanes(s, jnp.maximum))

            e = jnp.exp2(s_prv[:, pl.ds(off, tk)] - m_prev)
            ls_prv[...] += _fold_lanes(e, jnp.add)
            acc_prv[...] += jnp.dot(e.astype(MXU_DTYPE), v_ref[pl.ds(off, tk), :],
                                    preferred_element_type=F32)

            r0 = pl.multiple_of(kc * rows, rows)
            den = jnp.sum(ls_old[pl.ds(r0, rows), :], axis=-1, keepdims=True)
            finish(p, r0, acc_old[pl.ds(r0, rows), :] / den)
            return 0

        lax.fori_loop(0, nk, body, 0, unroll=_pick(nk, (4, 2)))

    for par in range(2):
        @pl.when(g % 2 == par)
        def _():
            run(par)


def _attn_scratch(tq, t):
    return [[pltpu.VMEM((tq, t), F32)] * 2] + [[pltpu.VMEM((tq, LANES), F32)] * 2] * 3


def _attn_units(units):
    stage = lambda d: (lambda g: jnp.clip(g - d, 0, units - 1))
    return stage(0), stage(1), stage(2)


def _mla_kernel(q_ref, k_ref, v_ref, o_ref, s_ref, mx_ref, ls_ref, acc_ref, *, tk):
    def finish(p, r0, o):
        sl = slice(p * MLA_V_DIM, (p + 1) * MLA_V_DIM)
        o_ref[pl.ds(r0, o.shape[0]), sl] = o[:, sl].astype(o_ref.dtype)

    _attn_step(pl.program_id(0), q_ref[...], k_ref, v_ref, s_ref, mx_ref, ls_ref, acc_ref, tk, finish)


def _mla_call(q, k, v, tq, tk):
    b, nh, t, _ = q.shape
    nq = t // tq
    units = b * nh * nq
    cur, prev, old = _attn_units(units)

    def unit(u):
        return u // (nh * nq), (u // (2 * nq)) % (nh // 2), (u // 2) % nq, u % 2

    def q_map(g):
        bi, p, i, h = unit(cur(g))
        return bi, 2 * p + h, i, 0

    def k_map(g):
        bi, p, _, h = unit(cur(g))
        return bi, 2 * p + h, 0, 0

    def v_map(g):
        bi, p, _, _ = unit(prev(g))
        return bi, 0, p

    def o_map(g):
        bi, p, i, _ = unit(old(g))
        return bi, i, p

    return pl.pallas_call(
        functools.partial(_mla_kernel, tk=tk),
        grid=(units + 2,),
        in_specs=[
            pl.BlockSpec((None, None, tq, LANES), q_map),
            pl.BlockSpec((None, None, t, LANES), k_map),
            pl.BlockSpec((None, t, LANES), v_map),
        ],
        out_specs=pl.BlockSpec((None, tq, LANES), o_map),
        out_shape=jax.ShapeDtypeStruct((b, t, D_MODEL), MXU_DTYPE),
        scratch_shapes=_attn_scratch(tq, t),
        compiler_params=pltpu.CompilerParams(dimension_semantics=("arbitrary",),
                                             vmem_limit_bytes=VMEM_LIMIT),
        name="mla_attn",
    )(q, k, v)


def _diff_kernel(sc_ref, q_ref, k_ref, v_ref, pq_ref, pk_ref, w_ref, o_ref,
                 s_ref, mx_ref, ls_ref, acc_ref, o1_ref, *, tk, nq, units, out_scale):
    g = pl.program_id(0)
    h_cur = (jnp.minimum(g, units - 1) // (2 * nq)) % DIFF_HEADS
    slope = sc_ref[h_cur]
    lam = sc_ref[DIFF_HEADS]
    pq = pq_ref[...] * slope
    w = w_ref[...]

    def penalty(off):
        return jnp.abs(pq - pk_ref[:, pl.ds(off, tk)] * slope)

    def finish(p, r0, o):
        rows = pl.ds(r0, o.shape[0])
        if p == 0:
            o1_ref[rows, :] = o
        else:
            od = o1_ref[rows, :] - lam * o
            o_ref[rows, :] = (_rms(od, w) * out_scale).astype(o_ref.dtype)

    _attn_step(g, q_ref[...], k_ref, v_ref, s_ref, mx_ref, ls_ref, acc_ref, tk, finish, penalty)


def _diff_call(scalars, qd, k, v, posq, posk, subw, tq, tk, out_scale):
    b, nh, _, t, _ = qd.shape
    nq = t // tq
    units = b * nh * nq * 2
    cur, prev, old = _attn_units(units)

    def unit(u):
        return u // (2 * nh * nq), (u // (2 * nq)) % nh, (u // 2) % nq, u % 2

    def q_map(g):
        bi, h, i, m = unit(cur(g))
        return bi, h, m, i, 0

    def k_map(g):
        bi, h, _, _ = unit(cur(g))
        return bi, h, 0, 0

    def v_map(g):
        bi, h, _, _ = unit(prev(g))
        return bi, 0, h

    def o_map(g):
        bi, h, i, _ = unit(old(g))
        return bi, i, h

    return pl.pallas_call(
        functools.partial(_diff_kernel, tk=tk, nq=nq, units=units, out_scale=out_scale),
        grid=(units + 2,),
        in_specs=[
            pl.BlockSpec(memory_space=pltpu.SMEM),
            pl.BlockSpec((None, None, None, tq, LANES), q_map),
            pl.BlockSpec((None, None, t, LANES), k_map),
            pl.BlockSpec((None, t, LANES), v_map),
            pl.BlockSpec((None, tq, 1), lambda g: (unit(cur(g))[0], unit(cur(g))[2], 0)),
            pl.BlockSpec((None, 1, t), lambda g: (unit(cur(g))[0], 0, 0)),
            pl.BlockSpec((1, LANES), lambda g: (0, 0)),
        ],
        out_specs=pl.BlockSpec((None, tq, LANES), o_map),
        out_shape=jax.ShapeDtypeStruct((b, t, D_MODEL), MXU_DTYPE),
        scratch_shapes=_attn_scratch(tq, t) + [pltpu.VMEM((tq, LANES), F32)],
        compiler_params=pltpu.CompilerParams(dimension_semantics=("arbitrary",),
                                             vmem_limit_bytes=VMEM_LIMIT),
        name="diff_attn",
    )(scalars, qd, k, v, posq, posk, subw)


def _split2(a):
    a1 = a.astype(jnp.bfloat16)
    return a1, (a - a1.astype(F32)).astype(jnp.bfloat16)


def _merge_kernel(x_ref, om_ref, od_ref, ga_ref, gb_ref, wo_ref, fnw_ref, rw1_ref, rw2_ref,
                  x1_out, aff_out, h2_out):
    merged = (ga_ref[...].astype(F32) * om_ref[...].astype(F32)
              + gb_ref[...].astype(F32) * od_ref[...].astype(F32))
    x1 = x_ref[...] + jnp.dot(merged.astype(MXU_DTYPE), wo_ref[...], preferred_element_type=F32)
    x1_out[...] = x1
    h2 = _rms(x1, fnw_ref[...])

    h2_out[...] = h2

    h1, hh2 = _split2(h2)
    w1, w2 = rw1_ref[...], rw2_ref[...]
    dot = lambda a, w: jnp.dot(a, w, preferred_element_type=F32)
    logits = (dot(hh2, w1) + dot(h1, w2)) + dot(h1, w1)
    lane = lax.broadcasted_iota(jnp.int32, (1, LANES), 1)
    logits = jnp.where(lane < N_EXPERTS, logits, -jnp.inf)
    e = jnp.exp(logits - jnp.max(logits, axis=-1, keepdims=True))
    aff = e / jnp.sum(e, axis=-1, keepdims=True)
    aff_out[...] = aff[:, :N_EXPERTS]


def _merge_call(x2d, om, od, ga, gb, wo, fnw, rw1, rw2, tm):
    n = x2d.shape[0]
    row = lambda i: (i, 0)
    act = pl.BlockSpec((tm, D_MODEL), row)
    return pl.pallas_call(
        _merge_kernel,
        grid=(n // tm,),
        in_specs=[act, act, act, act, act,
                  _const_spec((D_MODEL, D_MODEL)), _const_spec((1, D_MODEL)),
                  _const_spec((D_MODEL, LANES)), _const_spec((D_MODEL, LANES))],
        out_specs=[act, pl.BlockSpec((tm, N_EXPERTS), row), act],
        out_shape=[jax.ShapeDtypeStruct((n, D_MODEL), F32),
                   jax.ShapeDtypeStruct((n, N_EXPERTS), F32),
                   jax.ShapeDtypeStruct((n, D_MODEL), F32)],
        compiler_params=pltpu.CompilerParams(dimension_semantics=("arbitrary",),
                                             vmem_limit_bytes=VMEM_LIMIT),
        name="merge",
    )(x2d, om, od, ga, gb, wo, fnw, rw1, rw2)


def _route_kernel(a_ref, tri_ref, idx_out, gate_out, rank_ref, *, cap, jc):
    a = a_ref[...]
    ne, t = a.shape

    def search(_, lohi):
        lo, hi = lohi
        mid = lo + ((hi - lo + 1) >> 1)
        cnt = jnp.sum(jnp.where(a >= pltpu.bitcast(mid, F32), 1.0, 0.0), axis=-1, keepdims=True)
        ok = cnt >= cap
        return jnp.where(ok, mid, lo), jnp.where(ok, hi, mid - 1)

    lo0 = jnp.zeros((ne, 1), jnp.int32)
    hi0 = jnp.full((ne, 1), 0x7F800000, jnp.int32)
    thr_bits, _ = lax.fori_loop(0, 32, search, (lo0, hi0))
    thr = pltpu.bitcast(thr_bits, F32)

    tri = tri_ref[...]

    def prefix(mask):
        carry = jnp.zeros((ne, 1), F32)
        cols = []
        for c in range(t // LANES):
            m = mask[:, c * LANES:(c + 1) * LANES]
            cols.append(jnp.dot(m.astype(jnp.bfloat16), tri, preferred_element_type=F32) + carry)
            carry = carry + jnp.sum(m, axis=-1, keepdims=True)
        return jnp.concatenate(cols, axis=1), carry

    gt = jnp.where(a > thr, 1.0, 0.0)
    eq = jnp.where(a == thr, 1.0, 0.0)
    pre_eq, _ = prefix(eq)
    need = cap - jnp.sum(gt, axis=-1, keepdims=True)
    sel = jnp.maximum(gt, jnp.where(pre_eq < need, eq, 0.0))
    rank, _ = prefix(sel)
    rank_ref[...] = jnp.where(sel > 0.0, rank, -1.0)

    tok = lax.broadcasted_iota(jnp.int32, (1, t), 1).astype(F32)

    def per_expert(e, _):
        r = rank_ref[pl.ds(e, 1), :]
        av = a_ref[pl.ds(e, 1), :]

        def per_slots(c, _):
            j0 = pl.multiple_of(c * jc, jc)
            slot = (lax.broadcasted_iota(jnp.int32, (jc, 1), 0) + j0).astype(F32)
            hit = r == slot
            idx_out[e, pl.ds(j0, jc), :] = jnp.sum(jnp.where(hit, tok, 0.0), axis=-1,
                                                   keepdims=True).astype(jnp.int32)
            gate_out[e, pl.ds(j0, jc), :] = jnp.sum(jnp.where(hit, av, 0.0), axis=-1, keepdims=True)
            return 0

        return lax.fori_loop(0, cap // jc, per_slots, 0)

    lax.fori_loop(0, ne, per_expert, 0)


def _route_call(aff_t, tri, cap, jc):
    b, ne, t = aff_t.shape
    return pl.pallas_call(
        functools.partial(_route_kernel, cap=cap, jc=jc),
        grid=(b,),
        in_specs=[pl.BlockSpec((None, ne, t), lambda bi: (bi, 0, 0)), _const_spec((LANES, LANES))],
        out_specs=[pl.BlockSpec((None, ne, cap, 1), lambda bi: (bi, 0, 0, 0)),
                   pl.BlockSpec((None, ne, cap, 1), lambda bi: (bi, 0, 0, 0))],
        out_shape=[jax.ShapeDtypeStruct((b, ne, cap, 1), jnp.int32),
                   jax.ShapeDtypeStruct((b, ne, cap, 1), F32)],
        scratch_shapes=[pltpu.VMEM((ne, t), F32)],
        compiler_params=pltpu.CompilerParams(dimension_semantics=("arbitrary",),
                                             vmem_limit_bytes=VMEM_LIMIT),
        name="route",
    )(aff_t, tri)


_ROW_UNROLL = 8


_FF_CHUNK = 512


def _expert_kernel(idx_ref, idxn_ref, h2_hbm, gate_ref, wg_ref, wu_ref, wd_ref, y_out,
                   xe_ref, hid_ref, sem, *, cap):
    nb = pl.num_programs(1)
    step = pl.program_id(0) * nb + pl.program_id(1)
    last = pl.num_programs(0) * nb - 1
    cur = step % 2
    nxt = 1 - cur
    b_next = jnp.minimum(step + 1, last) % nb

    def row_copy(table_ref, b_src, slot, j):
        return pltpu.make_async_copy(h2_hbm.at[b_src, pl.ds(table_ref[0, j], 1), :],
                                     xe_ref.at[slot, pl.ds(j, 1), :], sem.at[slot])

    def wait_slot(slot):
        pltpu.make_async_copy(xe_ref.at[slot], xe_ref.at[slot], sem.at[slot]).wait()

    @pl.when(step == 0)
    def _():
        def first(j, _):
            row_copy(idx_ref, pl.program_id(1), 0, j).start()
            return 0

        lax.fori_loop(0, cap, first, 0)

    for j in range(cap):
        row_copy(idxn_ref, b_next, nxt, j).start()

    wait_slot(cur)
    xe = xe_ref[cur].astype(MXU_DTYPE)
    for c in range(EXPERT_FF // _FF_CHUNK):
        cols = slice(c * _FF_CHUNK, (c + 1) * _FF_CHUNK)
        g = jnp.dot(xe, wg_ref[:, cols], preferred_element_type=F32)
        u = jnp.dot(xe, wu_ref[:, cols], preferred_element_type=F32)
        hid_ref[:, cols] = (g * _sigmoid(g) * u).astype(MXU_DTYPE)
    y = jnp.dot(hid_ref[...], wd_ref[...], preferred_element_type=F32) * gate_ref[...]
    for c in range(D_MODEL // LANES):
        y_out[pl.ds(c, cap, stride=SUBLANES), :] = y[:, c * LANES:(c + 1) * LANES]

    @pl.when(step == last)
    def _():
        wait_slot(nxt)


def _expert_call(idx_smem, h2, gate, wg, wu, wd, cap):
    b, t, d = h2.shape
    ne = wg.shape[0]

    def next_idx(e, bi):
        s = jnp.minimum(e * b + bi + 1, ne * b - 1)
        return (s % b) * ne + s // b, 0, 0

    return pl.pallas_call(
        functools.partial(_expert_kernel, cap=cap),
        grid=(ne, b),
        in_specs=[
            pl.BlockSpec((None, 1, cap), lambda e, bi: (bi * ne + e, 0, 0), memory_space=pltpu.SMEM),
            pl.BlockSpec((None, 1, cap), next_idx, memory_space=pltpu.SMEM),
            pl.BlockSpec(memory_space=pl.ANY),
            pl.BlockSpec((None, None, cap, 1), lambda e, bi: (bi, e, 0, 0)),
            pl.BlockSpec((None, D_MODEL, EXPERT_FF), lambda e, bi: (e, 0, 0)),
            pl.BlockSpec((None, D_MODEL, EXPERT_FF), lambda e, bi: (e, 0, 0)),
            pl.BlockSpec((None, EXPERT_FF, D_MODEL), lambda e, bi: (e, 0, 0)),
        ],
        out_specs=pl.BlockSpec((None, None, cap * SUBLANES, LANES), lambda e, bi: (bi, e, 0, 0)),
        out_shape=jax.ShapeDtypeStruct((b, ne, cap * SUBLANES, LANES), F32),
        scratch_shapes=[pltpu.VMEM((2, cap, d), F32), pltpu.VMEM((cap, EXPERT_FF), MXU_DTYPE),
                        pltpu.SemaphoreType.DMA((2,))],
        compiler_params=pltpu.CompilerParams(dimension_semantics=("arbitrary",) * 2,
                                             vmem_limit_bytes=VMEM_LIMIT),
        name="experts",
    )(idx_smem, idx_smem, h2, gate, wg, wu, wd)


def _combine_kernel(idx_ref, x1_hbm, y_ref, o_hbm, acc_ref, stage_ref, sem, *, cap):
    bi, e = pl.program_id(0), pl.program_id(1)
    tile = lambda i: pl.ds(pl.multiple_of(i * SUBLANES, SUBLANES), SUBLANES)

    last = e == pl.num_programs(1) - 1
    rows = stage_ref.shape[1]
    n_blocks = acc_ref.shape[0] // (SUBLANES * rows)

    def load(r):
        return pltpu.make_async_copy(x1_hbm.at[bi, pl.ds(r * rows, rows), :], stage_ref.at[r % 2],
                                     sem.at[0, r % 2])

    def store(r):
        return pltpu.make_async_copy(stage_ref.at[r % 2], o_hbm.at[bi, pl.ds(r * rows, rows), :],
                                     sem.at[1, r % 2])

    @pl.when(e == 0)
    def _():
        acc_ref[...] = jnp.zeros_like(acc_ref)

    @pl.when(last)
    def _():
        load(0).start()

    def scatter(g, _):
        toks = [idx_ref[0, g * _ROW_UNROLL + u] for u in range(_ROW_UNROLL)]
        sums = [acc_ref[tile(toks[u]), :] + y_ref[tile(g * _ROW_UNROLL + u), :] for u in range(_ROW_UNROLL)]
        for u in range(_ROW_UNROLL):
            acc_ref[tile(toks[u]), :] = sums[u]
        return 0

    lax.fori_loop(0, cap // _ROW_UNROLL, scatter, 0)

    @pl.when(last)
    def _():
        for r in range(n_blocks):
            if r + 1 < n_blocks:
                if r >= 1:
                    store(r - 1).wait()
                load(r + 1).start()
            load(r).wait()
            for c in range(D_MODEL // LANES):
                stage_ref[r % 2, :, c * LANES:(c + 1) * LANES] += acc_ref[
                    pl.ds(r * rows * SUBLANES + c, rows, stride=SUBLANES), :]
            store(r).start()
        for r in range(max(n_blocks - 2, 0), n_blocks):
            store(r).wait()


def _combine_call(idx_smem, x1, y, cap):
    b, t, d = x1.shape
    ne = y.shape[1]
    return pl.pallas_call(
        functools.partial(_combine_kernel, cap=cap),
        grid=(b, ne),
        in_specs=[
            pl.BlockSpec((None, 1, cap), lambda bi, e: (bi * ne + e, 0, 0), memory_space=pltpu.SMEM),
            pl.BlockSpec(memory_space=pl.ANY),
            pl.BlockSpec((None, None, cap * SUBLANES, LANES), lambda bi, e: (bi, e, 0, 0)),
        ],
        out_specs=pl.BlockSpec(memory_space=pl.ANY),
        out_shape=jax.ShapeDtypeStruct((b, t, d), F32),
        scratch_shapes=[pltpu.VMEM((t * SUBLANES, LANES), F32),
                        pltpu.VMEM((2, _pick(t, (512, 256, 128)), d), F32),
                        pltpu.SemaphoreType.DMA((2, 2))],
        compiler_params=pltpu.CompilerParams(dimension_semantics=("arbitrary",) * 2,
                                             vmem_limit_bytes=VMEM_LIMIT),
        name="combine",
    )(idx_smem, x1, y)


def _pad_heads(w, n_heads, width):
    r = w.shape[0]
    w = w.reshape(r, n_heads, width)
    return jnp.pad(w, ((0, 0), (0, 0), (0, LANES - width))).reshape(r, n_heads * LANES)


def _pack_w_in(w_in):
    o = 0
    parts = {}
    for name, width in (("cq", MLA_Q_RANK), ("ckv", MLA_KV_RANK), ("kr", MLA_ROPE_DIM),
                        ("dq", D_MODEL), ("dk", D_MODEL), ("dv", D_MODEL), ("ga", D_MODEL), ("gb", D_MODEL)):
        parts[name] = w_in[:, o:o + width]
        o += width
    kr = jnp.pad(parts["kr"], ((0, 0), (MLA_NOPE_DIM, LANES - MLA_QK_DIM)))
    return jnp.concatenate([parts["cq"], parts["ckv"], kr, _rope_partner(kr), parts["dq"], parts["dk"],
                            parts["dv"], parts["ga"], parts["gb"]], axis=1).astype(MXU_DTYPE)


def _rope_partner(w):
    half = MLA_ROPE_DIM // 2
    lane = jnp.arange(LANES)
    src = jnp.where((lane >= MLA_NOPE_DIM) & (lane < MLA_NOPE_DIM + half), lane + half,
                    jnp.where((lane >= MLA_NOPE_DIM + half) & (lane < MLA_QK_DIM), lane - half, lane))
    blocks = w.reshape(w.shape[:-1] + (w.shape[-1] // LANES, LANES))
    return jnp.take(blocks, src, axis=-1).reshape(w.shape)


def _rope_tables(positions):
    inv_freq = 1.0 / (ROPE_THETA ** (jnp.arange(0, MLA_ROPE_DIM, 2, dtype=F32) / MLA_ROPE_DIM))
    ang = positions.astype(F32).reshape(-1, 1) * inv_freq
    cos, sin = jnp.cos(ang), jnp.sin(ang)
    n = ang.shape[0]
    z = lambda w: jnp.zeros((n, w), F32)
    tail = LANES - MLA_QK_DIM
    rc = jnp.concatenate([jnp.ones((n, MLA_NOPE_DIM), F32), cos, cos, z(tail)], axis=1)
    rs = jnp.concatenate([z(MLA_NOPE_DIM), -sin, sin, z(tail)], axis=1)
    return rc, rs


def _pick(t, prefs):
    for p in prefs:
        if t % p == 0:
            return p
    return t


def _layer(x, positions, layer_idx, attn_norm_w, w_in, b_gate, mla_q_norm_w, mla_w_uq, mla_kv_norm_w,
           mla_w_ukv, mla_q_hnorm_w, mla_k_hnorm_w, diff_q_hnorm_w, diff_k_hnorm_w, diff_lambda,
           diff_subln_w, w_out, ffn_norm_w, router_w, expert_w_gate, expert_w_up, expert_w_down):
    b, t, d = x.shape
    n = b * t
    cap = CAPACITY_FACTOR * t // N_EXPERTS
    row = lambda v: v.reshape(1, -1).astype(F32)

    win = _pack_w_in(w_in)
    wuq = _pad_heads(mla_w_uq, MLA_HEADS, MLA_QK_DIM)
    wuq = jnp.stack([wuq.reshape(MLA_Q_RANK, MLA_HEADS, LANES),
                     _rope_partner(wuq).reshape(MLA_Q_RANK, MLA_HEADS, LANES)], axis=2)
    wuq = wuq.reshape(MLA_Q_RANK, MLA_HEADS * 2 * LANES).astype(MXU_DTYPE)
    wukv = mla_w_ukv.reshape(MLA_KV_RANK, MLA_HEADS, MLA_NOPE_DIM + MLA_V_DIM)
    wuk = _pad_heads(wukv[:, :, :MLA_NOPE_DIM].reshape(MLA_KV_RANK, -1), MLA_HEADS, MLA_NOPE_DIM)
    wuv = wukv[:, :, MLA_NOPE_DIM:].reshape(MLA_KV_RANK, -1)
    wukv_p = jnp.concatenate([wuk, wuv], axis=1).astype(MXU_DTYPE)
    pad_gain = lambda w: jnp.pad(w.astype(F32), (0, LANES - MLA_QK_DIM)).reshape(1, LANES)
    qhw = pad_gain(mla_q_hnorm_w) * (MLA_QK_DIM ** -0.5 * LOG2E)
    khw = pad_gain(mla_k_hnorm_w)
    qhw = jnp.concatenate([qhw, _rope_partner(qhw)], axis=0)
    khw = jnp.concatenate([khw, _rope_partner(khw)], axis=0)
    dqw = jnp.tile(diff_q_hnorm_w.astype(F32), 2).reshape(1, LANES) * (DIFF_QK_DIM ** -0.5 * LOG2E)
    dkw = jnp.tile(diff_k_hnorm_w.astype(F32), 2).reshape(1, LANES)
    rc, rs = _rope_tables(positions)

    tm = _pick(t, (512, 256, 128))
    q, k, v, qd, kd, vd, ga, gb = _proj_call(
        x.reshape(n, d), b, t, row(attn_norm_w), win, row(mla_q_norm_w), wuq, row(mla_kv_norm_w), wukv_p,
        qhw, khw, dqw, dkw, row(b_gate), rc, rs, tm)

    tq = _pick(t, (1024, 512, 256, 128))
    tk = _pick(t, (512, 256, 128))
    o_mla = _mla_call(q, k, v.reshape(b, t, d), tq, tk)

    lam_init = 0.8 - 0.6 * math.exp(-0.3 * layer_idx)
    lf = diff_lambda.astype(F32)
    lam = jnp.exp(jnp.sum(lf[0] * lf[1])) - jnp.exp(jnp.sum(lf[2] * lf[3])) + lam_init
    slopes = jnp.asarray([2.0 ** (-8.0 * (i + 1) / DIFF_HEADS) * LOG2E for i in range(DIFF_HEADS)], F32)
    scalars = jnp.concatenate([slopes, lam.reshape(1)])
    posf = positions.astype(F32)
    o_diff = _diff_call(scalars, qd, kd, vd.reshape(b, t, d), posf.reshape(b, t, 1),
                        posf.reshape(b, 1, t), row(diff_subln_w), tq, tk, 1.0 - lam_init)

    rw = jnp.pad(router_w.astype(F32), ((0, 0), (0, LANES - N_EXPERTS)))
    rw1, rw2 = _split2(rw)
    tm2 = _pick(t, (512, 256, 128))
    x1, aff, h2 = _merge_call(x.reshape(n, d), o_mla.reshape(n, d), o_diff.reshape(n, d), ga, gb,
                              w_out.astype(MXU_DTYPE), row(ffn_norm_w), rw1, rw2, tm2)

    tri = (lax.broadcasted_iota(jnp.int32, (LANES, LANES), 0)
           < lax.broadcasted_iota(jnp.int32, (LANES, LANES), 1)).astype(jnp.bfloat16)
    aff_t = jnp.swapaxes(aff.reshape(b, t, N_EXPERTS), 1, 2)
    idx, gate = _route_call(aff_t, tri, cap, _pick(cap, (128, 64, 32, 16, 8)))
    idx_smem = idx.reshape(b * N_EXPERTS, 1, cap)

    y = _expert_call(idx_smem, h2.reshape(b, t, d), gate, expert_w_gate.astype(MXU_DTYPE),
                     expert_w_up.astype(MXU_DTYPE), expert_w_down.astype(MXU_DTYPE), cap)
    return _combine_call(idx_smem, x1.reshape(b, t, d), y, cap)


def kernel(x, positions, attn_norm_w, w_in, b_gate, mla_q_norm_w, mla_w_uq, mla_kv_norm_w, mla_w_ukv,
           mla_q_hnorm_w, mla_k_hnorm_w, diff_q_hnorm_w, diff_k_hnorm_w, diff_lambda, diff_subln_w, w_out,
           ffn_norm_w, router_w, expert_w_gate, expert_w_up, expert_w_down):
    for l in range(attn_norm_w.shape[0]):
        x = _layer(x, positions, l, attn_norm_w[l], w_in[l], b_gate[l], mla_q_norm_w[l], mla_w_uq[l],
                   mla_kv_norm_w[l], mla_w_ukv[l], mla_q_hnorm_w[l], mla_k_hnorm_w[l], diff_q_hnorm_w[l],
                   diff_k_hnorm_w[l], diff_lambda[l], diff_subln_w[l], w_out[l], ffn_norm_w[l], router_w[l],
                   expert_w_gate[l], expert_w_up[l], expert_w_down[l])
    return x
```

```python
import functools
import math

import jax
import jax.numpy as jnp
from jax import lax
from jax.experimental import pallas as pl
from jax.experimental.pallas import tpu as pltpu

F32 = jnp.float32
MXU_DTYPE = jnp.bfloat16

D_MODEL = 1024
MLA_HEADS = 16
MLA_Q_RANK = 256
MLA_KV_RANK = 256
MLA_NOPE_DIM = 64
MLA_ROPE_DIM = 32
MLA_QK_DIM = MLA_NOPE_DIM + MLA_ROPE_DIM
MLA_V_DIM = D_MODEL // MLA_HEADS
ROPE_THETA = 10000.0
DIFF_HEADS = 8
DIFF_QK_DIM = 64
DIFF_V_DIM = D_MODEL // DIFF_HEADS
N_EXPERTS = 16
CAPACITY_FACTOR = 2
EXPERT_FF = 2 * D_MODEL
EPS = 1e-6
LOG2E = math.log2(math.e)

LANES = 128
SUBLANES = 8
VMEM_LIMIT = 56 * 1024 * 1024

_SEG_CQ = (0, 256)
_SEG_CKV = (256, 512)
_SEG_KR = (512, 640)
_SEG_KRR = (640, 768)
_SEG_DQ = (768, 1792)
_SEG_DK = (1792, 2816)
_SEG_DV = (2816, 3840)
_SEG_GA = (3840, 4864)
_SEG_GB = (4864, 5888)
_IN_WIDTH_PACKED = 5888
_HEAD_GROUP = 4


def _rms(x, w):
    return x * lax.rsqrt(jnp.mean(x * x, axis=-1, keepdims=True) + EPS) * w


def _sigmoid(x):
    return 1.0 / (1.0 + jnp.exp(-x))


def _fold_lanes(x, op):
    parts = [x[:, i * LANES:(i + 1) * LANES] for i in range(x.shape[1] // LANES)]
    return functools.reduce(op, parts)


def _const_spec(shape):
    return pl.BlockSpec(shape, lambda *_: (0,) * len(shape), pipeline_mode=pl.Buffered(1))


def _proj_kernel(x_ref, anw_ref, win_ref, qnw_ref, wuq_ref, kvnw_ref, wukv_ref, qhw_ref, khw_ref,
                 dqw_ref, dkw_ref, bg_ref, rc_ref, rs_ref,
                 q_out, k_out, v_out, qd_out, kd_out, vd_out, ga_out, gb_out):
    x = x_ref[...]
    hb = _rms(x, anw_ref[...]).astype(MXU_DTYPE)

    def seg(ab):
        return jnp.dot(hb, win_ref[:, ab[0]:ab[1]], preferred_element_type=F32)

    rc, rs = rc_ref[...], rs_ref[...]
    qhw, khw = qhw_ref[...], khw_ref[...]
    cos_q, sin_q = rc * qhw[0:1], rs * qhw[1:2]
    cos_k, sin_k = rc * khw[0:1], rs * khw[1:2]

    def head_norm(v):
        return lax.rsqrt(jnp.sum(v * v, axis=-1, keepdims=True) * (1.0 / MLA_QK_DIM) + EPS)

    cqn = _rms(seg(_SEG_CQ), qnw_ref[...]).astype(MXU_DTYPE)
    gw = _HEAD_GROUP * 2 * LANES
    for grp in range(MLA_HEADS // _HEAD_GROUP):
        blk = jnp.dot(cqn, wuq_ref[:, grp * gw:(grp + 1) * gw], preferred_element_type=F32)
        for hh in range(_HEAD_GROUP):
            qh = blk[:, (2 * hh) * LANES:(2 * hh + 1) * LANES]
            qp = blk[:, (2 * hh + 1) * LANES:(2 * hh + 2) * LANES]
            q_out[grp * _HEAD_GROUP + hh] = ((qh * cos_q + qp * sin_q) * head_norm(qh)).astype(q_out.dtype)

    ckvn = _rms(seg(_SEG_CKV), kvnw_ref[...]).astype(MXU_DTYPE)
    kr = seg(_SEG_KR)
    krp = seg(_SEG_KRR) * sin_k
    v_out[...] = jnp.dot(ckvn, wukv_ref[:, MLA_HEADS * LANES:],
                         preferred_element_type=F32).astype(v_out.dtype)
    gw = _HEAD_GROUP * LANES
    for grp in range(MLA_HEADS // _HEAD_GROUP):
        blk = jnp.dot(ckvn, wukv_ref[:, grp * gw:(grp + 1) * gw], preferred_element_type=F32)
        for hh in range(_HEAD_GROUP):
            kh = blk[:, hh * LANES:(hh + 1) * LANES] + kr
            k_out[grp * _HEAD_GROUP + hh] = ((kh * cos_k + krp) * head_norm(kh)).astype(k_out.dtype)

    lo = lax.broadcasted_iota(jnp.int32, (1, LANES), 1) < DIFF_QK_DIM

    def halfnorm(blk, w):
        sq = blk * blk
        ss_lo = jnp.sum(jnp.where(lo, sq, 0.0), axis=-1, keepdims=True)
        ss_hi = jnp.sum(jnp.where(lo, 0.0, sq), axis=-1, keepdims=True)
        r = jnp.where(lo, lax.rsqrt(ss_lo * (1.0 / DIFF_QK_DIM) + EPS),
                      lax.rsqrt(ss_hi * (1.0 / DIFF_QK_DIM) + EPS))
        return blk * r * w

    dq = seg(_SEG_DQ)
    dqw = dqw_ref[...]
    for h in range(DIFF_HEADS):
        qn = halfnorm(dq[:, h * LANES:(h + 1) * LANES], dqw)
        qd_out[h, 0] = jnp.where(lo, qn, 0.0).astype(qd_out.dtype)
        qd_out[h, 1] = jnp.where(lo, 0.0, qn).astype(qd_out.dtype)
    dk = seg(_SEG_DK)
    dkw = dkw_ref[...]
    for h in range(DIFF_HEADS):
        kd_out[h] = halfnorm(dk[:, h * LANES:(h + 1) * LANES], dkw).astype(kd_out.dtype)
    vd_out[...] = seg(_SEG_DV).astype(vd_out.dtype)

    bg = bg_ref[...]
    ga_out[...] = _sigmoid(seg(_SEG_GA) + bg[:, :D_MODEL]).astype(ga_out.dtype)
    gb_out[...] = _sigmoid(seg(_SEG_GB) + bg[:, D_MODEL:]).astype(gb_out.dtype)


def _proj_call(x2d, b, t, anw, win, qnw, wuq, kvnw, wukv, qhw, khw, dqw, dkw, bg, rc, rs, tm):
    n = b * t
    tpb = t // tm
    row = lambda i: (i, 0)
    head = lambda i: (i // tpb, 0, i % tpb, 0)
    act = lambda: jax.ShapeDtypeStruct((n, D_MODEL), MXU_DTYPE)
    hd = lambda nh: jax.ShapeDtypeStruct((b, nh, t, LANES), MXU_DTYPE)
    return pl.pallas_call(
        _proj_kernel,
        grid=(n // tm,),
        in_specs=[
            pl.BlockSpec((tm, D_MODEL), row),
            _const_spec((1, D_MODEL)),
            _const_spec((D_MODEL, _IN_WIDTH_PACKED)),
            _const_spec((1, MLA_Q_RANK)),
            _const_spec((MLA_Q_RANK, MLA_HEADS * 2 * LANES)),
            _const_spec((1, MLA_KV_RANK)),
            _const_spec((MLA_KV_RANK, MLA_HEADS * LANES + D_MODEL)),
            _const_spec((2, LANES)), _const_spec((2, LANES)),
            _const_spec((1, LANES)), _const_spec((1, LANES)),
            _const_spec((1, 2 * D_MODEL)),
            pl.BlockSpec((tm, LANES), row), pl.BlockSpec((tm, LANES), row),
        ],
        out_specs=[
            pl.BlockSpec((None, MLA_HEADS, tm, LANES), head),
            pl.BlockSpec((None, MLA_HEADS, tm, LANES), head),
            pl.BlockSpec((tm, D_MODEL), row),
            pl.BlockSpec((None, DIFF_HEADS, 2, tm, LANES), lambda i: (i // tpb, 0, 0, i % tpb, 0)),
            pl.BlockSpec((None, DIFF_HEADS, tm, LANES), head),
            pl.BlockSpec((tm, D_MODEL), row),
            pl.BlockSpec((tm, D_MODEL), row),
            pl.BlockSpec((tm, D_MODEL), row),
        ],
        out_shape=[hd(MLA_HEADS), hd(MLA_HEADS), act(),
                   jax.ShapeDtypeStruct((b, DIFF_HEADS, 2, t, LANES), MXU_DTYPE), hd(DIFF_HEADS),
                   act(), act(), act()],
        compiler_params=pltpu.CompilerParams(dimension_semantics=("arbitrary",),
                                             vmem_limit_bytes=VMEM_LIMIT),
        name="proj",
    )(x2d, anw, win, qnw, wuq, kvnw, wukv, qhw, khw, dqw, dkw, bg, rc, rs)


_NT = (((1,), (1,)), ((), ()))


def _fold_rows(x, op):
    parts = [x[i * SUBLANES:(i + 1) * SUBLANES, :] for i in range(x.shape[0] // SUBLANES)]
    return functools.reduce(op, parts)


def _attn_step(g, q, k_ref, v_ref, s_ref, mx_ref, ls_ref, acc_ref, tk, finish, penalty_fn=None, *,
               keys_on_rows):
    tq = q.shape[0]
    nk = k_ref.shape[0] // tk
    span = tq // nk
    key_axis = 0 if keys_on_rows else 1
    fold = _fold_rows if keys_on_rows else _fold_lanes
    all_ = slice(None)
    keys = (lambda off: (pl.ds(off, tk), all_)) if keys_on_rows else (lambda off: (all_, pl.ds(off, tk)))
    qrys = (lambda c0: (all_, pl.ds(c0, span))) if keys_on_rows else (lambda c0: (pl.ds(c0, span), all_))

    @pl.when(g == 0)
    def _():
        s_ref[1][...] = jnp.zeros_like(s_ref[1])
        mx_ref[1][...] = jnp.zeros_like(mx_ref[1])
        acc_ref[0][...] = jnp.zeros_like(acc_ref[0])
        ls_ref[0][...] = jnp.ones_like(ls_ref[0])

    def run(p):
        s_cur, s_prv, mx_cur = s_ref[p], s_ref[1 - p], mx_ref[p]
        ls_prv, acc_prv, ls_old, acc_old = ls_ref[1 - p], acc_ref[1 - p], ls_ref[p], acc_ref[p]
        m_prev = jnp.max(mx_ref[1 - p][...], axis=key_axis, keepdims=True)
        mx_cur[...] = jnp.full(mx_cur.shape, -jnp.inf, F32)
        ls_prv[...] = jnp.zeros_like(ls_prv)
        acc_prv[...] = jnp.zeros_like(acc_prv)

        def body(kc, _):
            off = pl.multiple_of(kc * tk, tk)
            kblk = k_ref[pl.ds(off, tk), :]
            lhs, rhs = (kblk, q) if keys_on_rows else (q, kblk)
            s = lax.dot_general(lhs, rhs, _NT, preferred_element_type=F32)
            if penalty_fn is not None:
                s = s - penalty_fn(off)
            s_cur[keys(off)] = s
            mx_cur[...] = jnp.maximum(mx_cur[...], fold(s, jnp.maximum))

            e = jnp.exp2(s_prv[keys(off)] - m_prev)
            ls_prv[...] += fold(e, jnp.add)
            eb = e.astype(MXU_DTYPE)
            if keys_on_rows:
                acc_prv[...] += jnp.dot(v_ref[:, pl.ds(off, tk)], eb, preferred_element_type=F32)
            else:
                acc_prv[...] += jnp.dot(eb, v_ref[pl.ds(off, tk), :], preferred_element_type=F32)

            c0 = pl.multiple_of(kc * span, span)
            o = acc_old[qrys(c0)] / jnp.sum(ls_old[qrys(c0)], axis=key_axis, keepdims=True)
            finish(p, c0, o.T if keys_on_rows else o)
            return 0

        lax.fori_loop(0, nk, body, 0, unroll=_pick(nk, (4, 2)))

    for par in range(2):
        @pl.when(g % 2 == par)
        def _():
            run(par)


def _attn_scratch(tq, t, keys_on_rows):
    if keys_on_rows:
        shapes = [(t, tq), (SUBLANES, tq), (SUBLANES, tq), (LANES, tq)]
    else:
        shapes = [(tq, t), (tq, LANES), (tq, LANES), (tq, LANES)]
    return [[pltpu.VMEM(shape, F32)] * 2 for shape in shapes]


def _attn_units(units):
    stage = lambda d: (lambda g: jnp.clip(g - d, 0, units - 1))
    return stage(0), stage(1), stage(2)


def _mla_kernel(q_ref, k_ref, v_ref, o_ref, s_ref, mx_ref, ls_ref, acc_ref, *, tk):
    def finish(p, r0, o):
        sl = slice(p * MLA_V_DIM, (p + 1) * MLA_V_DIM)
        o_ref[pl.ds(r0, o.shape[0]), sl] = o[:, sl].astype(o_ref.dtype)

    _attn_step(pl.program_id(0), q_ref[...], k_ref, v_ref, s_ref, mx_ref, ls_ref, acc_ref, tk, finish,
               keys_on_rows=True)


def _mla_call(q, k, v, tq, tk):
    b, nh, t, _ = q.shape
    nq = t // tq
    units = b * nh * nq
    cur, prev, old = _attn_units(units)

    def unit(u):
        return u // (nh * nq), (u // (2 * nq)) % (nh // 2), (u // 2) % nq, u % 2

    def q_map(g):
        bi, p, i, h = unit(cur(g))
        return bi, 2 * p + h, i, 0

    def k_map(g):
        bi, p, _, h = unit(cur(g))
        return bi, 2 * p + h, 0, 0

    def v_map(g):
        bi, p, _, _ = unit(prev(g))
        return bi, p, 0

    def o_map(g):
        bi, p, i, _ = unit(old(g))
        return bi, i, p

    return pl.pallas_call(
        functools.partial(_mla_kernel, tk=tk),
        grid=(units + 2,),
        in_specs=[
            pl.BlockSpec((None, None, tq, LANES), q_map),
            pl.BlockSpec((None, None, t, LANES), k_map),
            pl.BlockSpec((None, LANES, t), v_map),
        ],
        out_specs=pl.BlockSpec((None, tq, LANES), o_map),
        out_shape=jax.ShapeDtypeStruct((b, t, D_MODEL), MXU_DTYPE),
        scratch_shapes=_attn_scratch(tq, t, True),
        compiler_params=pltpu.CompilerParams(dimension_semantics=("arbitrary",),
                                             vmem_limit_bytes=VMEM_LIMIT),
        name="mla_attn",
    )(q, k, v)


def _diff_kernel(sc_ref, q_ref, k_ref, v_ref, pq_ref, pk_ref, w_ref, o_ref,
                 s_ref, mx_ref, ls_ref, acc_ref, o1_ref, *, tk, nq, units, out_scale):
    g = pl.program_id(0)
    h_cur = (jnp.minimum(g, units - 1) // (2 * nq)) % DIFF_HEADS
    slope = sc_ref[h_cur]
    lam = sc_ref[DIFF_HEADS]
    pq = pq_ref[...] * slope
    w = w_ref[...]

    def penalty(off):
        return jnp.abs(pq - pk_ref[:, pl.ds(off, tk)] * slope)

    def finish(p, r0, o):
        rows = pl.ds(r0, o.shape[0])
        if p == 0:
            o1_ref[rows, :] = o
        else:
            od = o1_ref[rows, :] - lam * o
            o_ref[rows, :] = (_rms(od, w) * out_scale).astype(o_ref.dtype)

    _attn_step(g, q_ref[...], k_ref, v_ref, s_ref, mx_ref, ls_ref, acc_ref, tk, finish, penalty,
               keys_on_rows=False)


def _diff_call(scalars, qd, k, v, posq, posk, subw, tq, tk, out_scale):
    b, nh, _, t, _ = qd.shape
    nq = t // tq
    units = b * nh * nq * 2
    cur, prev, old = _attn_units(units)

    def unit(u):
        return u // (2 * nh * nq), (u // (2 * nq)) % nh, (u // 2) % nq, u % 2

    def q_map(g):
        bi, h, i, m = unit(cur(g))
        return bi, h, m, i, 0

    def k_map(g):
        bi, h, _, _ = unit(cur(g))
        return bi, h, 0, 0

    def v_map(g):
        bi, h, _, _ = unit(prev(g))
        return bi, 0, h

    def o_map(g):
        bi, h, i, _ = unit(old(g))
        return bi, i, h

    return pl.pallas_call(
        functools.partial(_diff_kernel, tk=tk, nq=nq, units=units, out_scale=out_scale),
        grid=(units + 2,),
        in_specs=[
            pl.BlockSpec(memory_space=pltpu.SMEM),
            pl.BlockSpec((None, None, None, tq, LANES), q_map),
            pl.BlockSpec((None, None, t, LANES), k_map),
            pl.BlockSpec((None, t, LANES), v_map),
            pl.BlockSpec((None, tq, 1), lambda g: (unit(cur(g))[0], unit(cur(g))[2], 0)),
            pl.BlockSpec((None, 1, t), lambda g: (unit(cur(g))[0], 0, 0)),
            pl.BlockSpec((1, LANES), lambda g: (0, 0)),
        ],
        out_specs=pl.BlockSpec((None, tq, LANES), o_map),
        out_shape=jax.ShapeDtypeStruct((b, t, D_MODEL), MXU_DTYPE),
        scratch_shapes=_attn_scratch(tq, t, False) + [pltpu.VMEM((tq, LANES), F32)],
        compiler_params=pltpu.CompilerParams(dimension_semantics=("arbitrary",),
                                             vmem_limit_bytes=VMEM_LIMIT),
        name="diff_attn",
    )(scalars, qd, k, v, posq, posk, subw)


def _split2(a):
    a1 = a.astype(jnp.bfloat16)
    return a1, (a - a1.astype(F32)).astype(jnp.bfloat16)


def _merge_kernel(x_ref, om_ref, od_ref, ga_ref, gb_ref, wo_ref, fnw_ref, rw1_ref, rw2_ref,
                  x1_out, aff_out, h2_out):
    merged = (ga_ref[...].astype(F32) * om_ref[...].astype(F32)
              + gb_ref[...].astype(F32) * od_ref[...].astype(F32))
    x1 = x_ref[...] + jnp.dot(merged.astype(MXU_DTYPE), wo_ref[...], preferred_element_type=F32)
    x1_out[...] = x1
    h2 = _rms(x1, fnw_ref[...])

    h2_out[...] = h2

    h1, hh2 = _split2(h2)
    w1, w2 = rw1_ref[...], rw2_ref[...]
    dot = lambda a, w: jnp.dot(a, w, preferred_element_type=F32)
    logits = (dot(hh2, w1) + dot(h1, w2)) + dot(h1, w1)
    lane = lax.broadcasted_iota(jnp.int32, (1, LANES), 1)
    logits = jnp.where(lane < N_EXPERTS, logits, -jnp.inf)
    e = jnp.exp(logits - jnp.max(logits, axis=-1, keepdims=True))
    aff = e / jnp.sum(e, axis=-1, keepdims=True)
    aff_out[...] = aff[:, :N_EXPERTS]


def _merge_call(x2d, om, od, ga, gb, wo, fnw, rw1, rw2, tm):
    n = x2d.shape[0]
    row = lambda i: (i, 0)
    act = pl.BlockSpec((tm, D_MODEL), row)
    return pl.pallas_call(
        _merge_kernel,
        grid=(n // tm,),
        in_specs=[act, act, act, act, act,
                  _const_spec((D_MODEL, D_MODEL)), _const_spec((1, D_MODEL)),
                  _const_spec((D_MODEL, LANES)), _const_spec((D_MODEL, LANES))],
        out_specs=[act, pl.BlockSpec((tm, N_EXPERTS), row), act],
        out_shape=[jax.ShapeDtypeStruct((n, D_MODEL), F32),
                   jax.ShapeDtypeStruct((n, N_EXPERTS), F32),
                   jax.ShapeDtypeStruct((n, D_MODEL), F32)],
        compiler_params=pltpu.CompilerParams(dimension_semantics=("arbitrary",),
                                             vmem_limit_bytes=VMEM_LIMIT),
        name="merge",
    )(x2d, om, od, ga, gb, wo, fnw, rw1, rw2)


def _route_kernel(a_ref, tri_ref, idx_out, gate_out, rank_ref, *, cap, jc):
    a = a_ref[...]
    ne, t = a.shape

    def search(_, lohi):
        lo, hi = lohi
        mid = lo + ((hi - lo + 1) >> 1)
        cnt = jnp.sum(jnp.where(a >= pltpu.bitcast(mid, F32), 1.0, 0.0), axis=-1, keepdims=True)
        ok = cnt >= cap
        return jnp.where(ok, mid, lo), jnp.where(ok, hi, mid - 1)

    lo0 = jnp.zeros((ne, 1), jnp.int32)
    hi0 = jnp.full((ne, 1), 0x7F800000, jnp.int32)
    thr_bits, _ = lax.fori_loop(0, 32, search, (lo0, hi0))
    thr = pltpu.bitcast(thr_bits, F32)

    tri = tri_ref[...]

    def prefix(mask):
        carry = jnp.zeros((ne, 1), F32)
        cols = []
        for c in range(t // LANES):
            m = mask[:, c * LANES:(c + 1) * LANES]
            cols.append(jnp.dot(m.astype(jnp.bfloat16), tri, preferred_element_type=F32) + carry)
            carry = carry + jnp.sum(m, axis=-1, keepdims=True)
        return jnp.concatenate(cols, axis=1), carry

    gt = jnp.where(a > thr, 1.0, 0.0)
    eq = jnp.where(a == thr, 1.0, 0.0)
    pre_eq, _ = prefix(eq)
    need = cap - jnp.sum(gt, axis=-1, keepdims=True)
    sel = jnp.maximum(gt, jnp.where(pre_eq < need, eq, 0.0))
    rank, _ = prefix(sel)
    rank_ref[...] = jnp.where(sel > 0.0, rank, -1.0)

    tok = lax.broadcasted_iota(jnp.int32, (1, t), 1).astype(F32)

    def per_expert(e, _):
        r = rank_ref[pl.ds(e, 1), :]
        av = a_ref[pl.ds(e, 1), :]

        def per_slots(c, _):
            j0 = pl.multiple_of(c * jc, jc)
            slot = (lax.broadcasted_iota(jnp.int32, (jc, 1), 0) + j0).astype(F32)
            hit = r == slot
            idx_out[e, pl.ds(j0, jc), :] = jnp.sum(jnp.where(hit, tok, 0.0), axis=-1,
                                                   keepdims=True).astype(jnp.int32)
            gate_out[e, pl.ds(j0, jc), :] = jnp.sum(jnp.where(hit, av, 0.0), axis=-1, keepdims=True)
            return 0

        return lax.fori_loop(0, cap // jc, per_slots, 0)

    lax.fori_loop(0, ne, per_expert, 0)


def _route_call(aff_t, tri, cap, jc):
    b, ne, t = aff_t.shape
    return pl.pallas_call(
        functools.partial(_route_kernel, cap=cap, jc=jc),
        grid=(b,),
        in_specs=[pl.BlockSpec((None, ne, t), lambda bi: (bi, 0, 0)), _const_spec((LANES, LANES))],
        out_specs=[pl.BlockSpec((None, ne, cap, 1), lambda bi: (bi, 0, 0, 0)),
                   pl.BlockSpec((None, ne, cap, 1), lambda bi: (bi, 0, 0, 0))],
        out_shape=[jax.ShapeDtypeStruct((b, ne, cap, 1), jnp.int32),
                   jax.ShapeDtypeStruct((b, ne, cap, 1), F32)],
        scratch_shapes=[pltpu.VMEM((ne, t), F32)],
        compiler_params=pltpu.CompilerParams(dimension_semantics=("arbitrary",),
                                             vmem_limit_bytes=VMEM_LIMIT),
        name="route",
    )(aff_t, tri)


_ROW_UNROLL = 8


_FF_CHUNK = 512


def _expert_kernel(idx_ref, idxn_ref, h2_hbm, gate_ref, wg_ref, wu_ref, wd_ref, y_out,
                   xe_ref, hid_ref, sem, *, cap):
    nb = pl.num_programs(1)
    step = pl.program_id(0) * nb + pl.program_id(1)
    last = pl.num_programs(0) * nb - 1
    cur = step % 2
    nxt = 1 - cur
    b_next = jnp.minimum(step + 1, last) % nb

    def row_copy(table_ref, b_src, slot, j):
        return pltpu.make_async_copy(h2_hbm.at[b_src, pl.ds(table_ref[0, j], 1), :],
                                     xe_ref.at[slot, pl.ds(j, 1), :], sem.at[slot])

    def wait_slot(slot):
        pltpu.make_async_copy(xe_ref.at[slot], xe_ref.at[slot], sem.at[slot]).wait()

    @pl.when(step == 0)
    def _():
        def first(j, _):
            row_copy(idx_ref, pl.program_id(1), 0, j).start()
            return 0

        lax.fori_loop(0, cap, first, 0)

    for j in range(cap):
        row_copy(idxn_ref, b_next, nxt, j).start()

    wait_slot(cur)
    xe = xe_ref[cur].astype(MXU_DTYPE)
    for c in range(EXPERT_FF // _FF_CHUNK):
        cols = slice(c * _FF_CHUNK, (c + 1) * _FF_CHUNK)
        g = jnp.dot(xe, wg_ref[:, cols], preferred_element_type=F32)
        u = jnp.dot(xe, wu_ref[:, cols], preferred_element_type=F32)
        hid_ref[:, cols] = (g * _sigmoid(g) * u).astype(MXU_DTYPE)
    y = jnp.dot(hid_ref[...], wd_ref[...], preferred_element_type=F32) * gate_ref[...]
    for c in range(D_MODEL // LANES):
        y_out[pl.ds(c, cap, stride=SUBLANES), :] = y[:, c * LANES:(c + 1) * LANES]

    @pl.when(step == last)
    def _():
        wait_slot(nxt)


def _expert_call(idx_smem, h2, gate, wg, wu, wd, cap):
    b, t, d = h2.shape
    ne = wg.shape[0]

    def next_idx(e, bi):
        s = jnp.minimum(e * b + bi + 1, ne * b - 1)
        return (s % b) * ne + s // b, 0, 0

    return pl.pallas_call(
        functools.partial(_expert_kernel, cap=cap),
        grid=(ne, b),
        in_specs=[
            pl.BlockSpec((None, 1, cap), lambda e, bi: (bi * ne + e, 0, 0), memory_space=pltpu.SMEM),
            pl.BlockSpec((None, 1, cap), next_idx, memory_space=pltpu.SMEM),
            pl.BlockSpec(memory_space=pl.ANY),
            pl.BlockSpec((None, None, cap, 1), lambda e, bi: (bi, e, 0, 0)),
            pl.BlockSpec((None, D_MODEL, EXPERT_FF), lambda e, bi: (e, 0, 0)),
            pl.BlockSpec((None, D_MODEL, EXPERT_FF), lambda e, bi: (e, 0, 0)),
            pl.BlockSpec((None, EXPERT_FF, D_MODEL), lambda e, bi: (e, 0, 0)),
        ],
        out_specs=pl.BlockSpec((None, None, cap * SUBLANES, LANES), lambda e, bi: (bi, e, 0, 0)),
        out_shape=jax.ShapeDtypeStruct((b, ne, cap * SUBLANES, LANES), F32),
        scratch_shapes=[pltpu.VMEM((2, cap, d), F32), pltpu.VMEM((cap, EXPERT_FF), MXU_DTYPE),
                        pltpu.SemaphoreType.DMA((2,))],
        compiler_params=pltpu.CompilerParams(dimension_semantics=("arbitrary",) * 2,
                                             vmem_limit_bytes=VMEM_LIMIT),
        name="experts",
    )(idx_smem, idx_smem, h2, gate, wg, wu, wd)


def _combine_kernel(idx_ref, x1_hbm, y_ref, o_hbm, acc_ref, stage_ref, sem, *, cap):
    bi, e = pl.program_id(0), pl.program_id(1)
    tile = lambda i: pl.ds(pl.multiple_of(i * SUBLANES, SUBLANES), SUBLANES)

    last = e == pl.num_programs(1) - 1
    rows = stage_ref.shape[1]
    n_blocks = acc_ref.shape[0] // (SUBLANES * rows)

    def load(r):
        return pltpu.make_async_copy(x1_hbm.at[bi, pl.ds(r * rows, rows), :], stage_ref.at[r % 2],
                                     sem.at[0, r % 2])

    def store(r):
        return pltpu.make_async_copy(stage_ref.at[r % 2], o_hbm.at[bi, pl.ds(r * rows, rows), :],
                                     sem.at[1, r % 2])

    @pl.when(e == 0)
    def _():
        acc_ref[...] = jnp.zeros_like(acc_ref)

    @pl.when(last)
    def _():
        load(0).start()

    def scatter(g, _):
        toks = [idx_ref[0, g * _ROW_UNROLL + u] for u in range(_ROW_UNROLL)]
        sums = [acc_ref[tile(toks[u]), :] + y_ref[tile(g * _ROW_UNROLL + u), :] for u in range(_ROW_UNROLL)]
        for u in range(_ROW_UNROLL):
            acc_ref[tile(toks[u]), :] = sums[u]
        return 0

    lax.fori_loop(0, cap // _ROW_UNROLL, scatter, 0)

    @pl.when(last)
    def _():
        for r in range(n_blocks):
            if r + 1 < n_blocks:
                if r >= 1:
                    store(r - 1).wait()
                load(r + 1).start()
            load(r).wait()
            for c in range(D_MODEL // LANES):
                stage_ref[r % 2, :, c * LANES:(c + 1) * LANES] += acc_ref[
                    pl.ds(r * rows * SUBLANES + c, rows, stride=SUBLANES), :]
            store(r).start()
        for r in range(max(n_blocks - 2, 0), n_blocks):
            store(r).wait()


def _combine_call(idx_smem, x1, y, cap):
    b, t, d = x1.shape
    ne = y.shape[1]
    return pl.pallas_call(
        functools.partial(_combine_kernel, cap=cap),
        grid=(b, ne),
        in_specs=[
            pl.BlockSpec((None, 1, cap), lambda bi, e: (bi * ne + e, 0, 0), memory_space=pltpu.SMEM),
            pl.BlockSpec(memory_space=pl.ANY),
            pl.BlockSpec((None, None, cap * SUBLANES, LANES), lambda bi, e: (bi, e, 0, 0)),
        ],
        out_specs=pl.BlockSpec(memory_space=pl.ANY),
        out_shape=jax.ShapeDtypeStruct((b, t, d), F32),
        scratch_shapes=[pltpu.VMEM((t * SUBLANES, LANES), F32),
                        pltpu.VMEM((2, _pick(t, (512, 256, 128)), d), F32),
                        pltpu.SemaphoreType.DMA((2, 2))],
        compiler_params=pltpu.CompilerParams(dimension_semantics=("arbitrary",) * 2,
                                             vmem_limit_bytes=VMEM_LIMIT),
        name="combine",
    )(idx_smem, x1, y)


def _pad_heads(w, n_heads, width):
    r = w.shape[0]
    w = w.reshape(r, n_heads, width)
    return jnp.pad(w, ((0, 0), (0, 0), (0, LANES - width))).reshape(r, n_heads * LANES)


def _pack_w_in(w_in):
    o = 0
    parts = {}
    for name, width in (("cq", MLA_Q_RANK), ("ckv", MLA_KV_RANK), ("kr", MLA_ROPE_DIM),
                        ("dq", D_MODEL), ("dk", D_MODEL), ("dv", D_MODEL), ("ga", D_MODEL), ("gb", D_MODEL)):
        parts[name] = w_in[:, o:o + width]
        o += width
    kr = jnp.pad(parts["kr"], ((0, 0), (MLA_NOPE_DIM, LANES - MLA_QK_DIM)))
    return jnp.concatenate([parts["cq"], parts["ckv"], kr, _rope_partner(kr), parts["dq"], parts["dk"],
                            parts["dv"], parts["ga"], parts["gb"]], axis=1).astype(MXU_DTYPE)


def _rope_partner(w):
    half = MLA_ROPE_DIM // 2
    lane = jnp.arange(LANES)
    src = jnp.where((lane >= MLA_NOPE_DIM) & (lane < MLA_NOPE_DIM + half), lane + half,
                    jnp.where((lane >= MLA_NOPE_DIM + half) & (lane < MLA_QK_DIM), lane - half, lane))
    blocks = w.reshape(w.shape[:-1] + (w.shape[-1] // LANES, LANES))
    return jnp.take(blocks, src, axis=-1).reshape(w.shape)


def _rope_tables(positions):
    inv_freq = 1.0 / (ROPE_THETA ** (jnp.arange(0, MLA_ROPE_DIM, 2, dtype=F32) / MLA_ROPE_DIM))
    ang = positions.astype(F32).reshape(-1, 1) * inv_freq
    cos, sin = jnp.cos(ang), jnp.sin(ang)
    n = ang.shape[0]
    z = lambda w: jnp.zeros((n, w), F32)
    tail = LANES - MLA_QK_DIM
    rc = jnp.concatenate([jnp.ones((n, MLA_NOPE_DIM), F32), cos, cos, z(tail)], axis=1)
    rs = jnp.concatenate([z(MLA_NOPE_DIM), -sin, sin, z(tail)], axis=1)
    return rc, rs


def _pick(t, prefs):
    for p in prefs:
        if t % p == 0:
            return p
    return t


def _layer(x, positions, layer_idx, attn_norm_w, w_in, b_gate, mla_q_norm_w, mla_w_uq, mla_kv_norm_w,
           mla_w_ukv, mla_q_hnorm_w, mla_k_hnorm_w, diff_q_hnorm_w, diff_k_hnorm_w, diff_lambda,
           diff_subln_w, w_out, ffn_norm_w, router_w, expert_w_gate, expert_w_up, expert_w_down):
    b, t, d = x.shape
    n = b * t
    cap = CAPACITY_FACTOR * t // N_EXPERTS
    row = lambda v: v.reshape(1, -1).astype(F32)

    win = _pack_w_in(w_in)
    wuq = _pad_heads(mla_w_uq, MLA_HEADS, MLA_QK_DIM)
    wuq = jnp.stack([wuq.reshape(MLA_Q_RANK, MLA_HEADS, LANES),
                     _rope_partner(wuq).reshape(MLA_Q_RANK, MLA_HEADS, LANES)], axis=2)
    wuq = wuq.reshape(MLA_Q_RANK, MLA_HEADS * 2 * LANES).astype(MXU_DTYPE)
    wukv = mla_w_ukv.reshape(MLA_KV_RANK, MLA_HEADS, MLA_NOPE_DIM + MLA_V_DIM)
    wuk = _pad_heads(wukv[:, :, :MLA_NOPE_DIM].reshape(MLA_KV_RANK, -1), MLA_HEADS, MLA_NOPE_DIM)
    wuv = wukv[:, :, MLA_NOPE_DIM:].reshape(MLA_KV_RANK, -1)
    wukv_p = jnp.concatenate([wuk, wuv], axis=1).astype(MXU_DTYPE)
    pad_gain = lambda w: jnp.pad(w.astype(F32), (0, LANES - MLA_QK_DIM)).reshape(1, LANES)
    qhw = pad_gain(mla_q_hnorm_w) * (MLA_QK_DIM ** -0.5 * LOG2E)
    khw = pad_gain(mla_k_hnorm_w)
    qhw = jnp.concatenate([qhw, _rope_partner(qhw)], axis=0)
    khw = jnp.concatenate([khw, _rope_partner(khw)], axis=0)
    dqw = jnp.tile(diff_q_hnorm_w.astype(F32), 2).reshape(1, LANES) * (DIFF_QK_DIM ** -0.5 * LOG2E)
    dkw = jnp.tile(diff_k_hnorm_w.astype(F32), 2).reshape(1, LANES)
    rc, rs = _rope_tables(positions)

    tm = _pick(t, (512, 256, 128))
    q, k, v, qd, kd, vd, ga, gb = _proj_call(
        x.reshape(n, d), b, t, row(attn_norm_w), win, row(mla_q_norm_w), wuq, row(mla_kv_norm_w), wukv_p,
        qhw, khw, dqw, dkw, row(b_gate), rc, rs, tm)

    tq = _pick(t, (1024, 512, 256, 128))
    tk = _pick(t, (512, 256, 128))
    o_mla = _mla_call(q, k, jnp.swapaxes(v.reshape(b, t, d), 1, 2), tq, tk)

    lam_init = 0.8 - 0.6 * math.exp(-0.3 * layer_idx)
    lf = diff_lambda.astype(F32)
    lam = jnp.exp(jnp.sum(lf[0] * lf[1])) - jnp.exp(jnp.sum(lf[2] * lf[3])) + lam_init
    slopes = jnp.asarray([2.0 ** (-8.0 * (i + 1) / DIFF_HEADS) * LOG2E for i in range(DIFF_HEADS)], F32)
    scalars = jnp.concatenate([slopes, lam.reshape(1)])
    posf = positions.astype(F32)
    o_diff = _diff_call(scalars, qd, kd, vd.reshape(b, t, d), posf.reshape(b, t, 1),
                        posf.reshape(b, 1, t), row(diff_subln_w), tq, tk, 1.0 - lam_init)

    rw = jnp.pad(router_w.astype(F32), ((0, 0), (0, LANES - N_EXPERTS)))
    rw1, rw2 = _split2(rw)
    tm2 = _pick(t, (512, 256, 128))
    x1, aff, h2 = _merge_call(x.reshape(n, d), o_mla.reshape(n, d), o_diff.reshape(n, d), ga, gb,
                              w_out.astype(MXU_DTYPE), row(ffn_norm_w), rw1, rw2, tm2)

    tri = (lax.broadcasted_iota(jnp.int32, (LANES, LANES), 0)
           < lax.broadcasted_iota(jnp.int32, (LANES, LANES), 1)).astype(jnp.bfloat16)
    aff_t = jnp.swapaxes(aff.reshape(b, t, N_EXPERTS), 1, 2)
    idx, gate = _route_call(aff_t, tri, cap, _pick(cap, (128, 64, 32, 16, 8)))
    idx_smem = idx.reshape(b * N_EXPERTS, 1, cap)

    y = _expert_call(idx_smem, h2.reshape(b, t, d), gate, expert_w_gate.astype(MXU_DTYPE),
                     expert_w_up.astype(MXU_DTYPE), expert_w_down.astype(MXU_DTYPE), cap)
    return _combine_call(idx_smem, x1.reshape(b, t, d), y, cap)


def kernel(x, positions, attn_norm_w, w_in, b_gate, mla_q_norm_w, mla_w_uq, mla_kv_norm_w, mla_w_ukv,
           mla_q_hnorm_w, mla_k_hnorm_w, diff_q_hnorm_w, diff_k_hnorm_w, diff_lambda, diff_subln_w, w_out,
           ffn_norm_w, router_w, expert_w_gate, expert_w_up, expert_w_down):
    for l in range(attn_norm_w.shape[0]):
        x = _layer(x, positions, l, attn_norm_w[l], w_in[l], b_gate[l], mla_q_norm_w[l], mla_w_uq[l],
                   mla_kv_norm_w[l], mla_w_ukv[l], mla_q_hnorm_w[l], mla_k_hnorm_w[l], diff_q_hnorm_w[l],
                   diff_k_hnorm_w[l], diff_lambda[l], diff_subln_w[l], w_out[l], ffn_norm_w[l], router_w[l],
                   expert_w_gate[l], expert_w_up[l], expert_w_down[l])
    return x
```

```python
import functools
import math

import jax
import jax.numpy as jnp
from jax import lax
from jax.experimental import pallas as pl
from jax.experimental.pallas import tpu as pltpu

F32 = jnp.float32
MXU_DTYPE = jnp.bfloat16

D_MODEL = 1024
MLA_HEADS = 16
MLA_Q_RANK = 256
MLA_KV_RANK = 256
MLA_NOPE_DIM = 64
MLA_ROPE_DIM = 32
MLA_QK_DIM = MLA_NOPE_DIM + MLA_ROPE_DIM
MLA_V_DIM = D_MODEL // MLA_HEADS
ROPE_THETA = 10000.0
DIFF_HEADS = 8
DIFF_QK_DIM = 64
DIFF_V_DIM = D_MODEL // DIFF_HEADS
N_EXPERTS = 16
CAPACITY_FACTOR = 2
EXPERT_FF = 2 * D_MODEL
EPS = 1e-6
LOG2E = math.log2(math.e)

LANES = 128
SUBLANES = 8
VMEM_LIMIT = 56 * 1024 * 1024

_SEG_CQ = (0, 256)
_SEG_CKV = (256, 512)
_SEG_KR = (512, 640)
_SEG_KRR = (640, 768)
_SEG_DQ = (768, 1792)
_SEG_DK = (1792, 2816)
_SEG_DV = (2816, 3840)
_SEG_GA = (3840, 4864)
_SEG_GB = (4864, 5888)
_IN_WIDTH_PACKED = 5888
_HEAD_GROUP = 4


def _rms(x, w):
    return x * lax.rsqrt(jnp.mean(x * x, axis=-1, keepdims=True) + EPS) * w


def _sigmoid(x):
    return 1.0 / (1.0 + jnp.exp(-x))


def _fold_lanes(x, op):
    parts = [x[:, i * LANES:(i + 1) * LANES] for i in range(x.shape[1] // LANES)]
    return functools.reduce(op, parts)


def _const_spec(shape):
    return pl.BlockSpec(shape, lambda *_: (0,) * len(shape), pipeline_mode=pl.Buffered(1))


def _proj_kernel(x_ref, anw_ref, win_ref, qnw_ref, wuq_ref, kvnw_ref, wukv_ref, qhw_ref, khw_ref,
                 dqw_ref, dkw_ref, bg_ref, rc_ref, rs_ref,
                 q_out, k_out, v_out, qd_out, kd_out, vd_out, ga_out, gb_out):
    x = x_ref[...]
    hb = _rms(x, anw_ref[...]).astype(MXU_DTYPE)

    def seg(ab):
        return jnp.dot(hb, win_ref[:, ab[0]:ab[1]], preferred_element_type=F32)

    rc, rs = rc_ref[...], rs_ref[...]
    qhw, khw = qhw_ref[...], khw_ref[...]
    cos_q, sin_q = rc * qhw[0:1], rs * qhw[1:2]
    cos_k, sin_k = rc * khw[0:1], rs * khw[1:2]

    def head_norm(v):
        return lax.rsqrt(jnp.sum(v * v, axis=-1, keepdims=True) * (1.0 / MLA_QK_DIM) + EPS)

    cqn = _rms(seg(_SEG_CQ), qnw_ref[...]).astype(MXU_DTYPE)
    gw = _HEAD_GROUP * 2 * LANES
    for grp in range(MLA_HEADS // _HEAD_GROUP):
        blk = jnp.dot(cqn, wuq_ref[:, grp * gw:(grp + 1) * gw], preferred_element_type=F32)
        for hh in range(_HEAD_GROUP):
            qh = blk[:, (2 * hh) * LANES:(2 * hh + 1) * LANES]
            qp = blk[:, (2 * hh + 1) * LANES:(2 * hh + 2) * LANES]
            q_out[grp * _HEAD_GROUP + hh] = ((qh * cos_q + qp * sin_q) * head_norm(qh)).astype(q_out.dtype)

    ckvn = _rms(seg(_SEG_CKV), kvnw_ref[...]).astype(MXU_DTYPE)
    kr = seg(_SEG_KR)
    krp = seg(_SEG_KRR) * sin_k
    v_out[...] = jnp.dot(ckvn, wukv_ref[:, MLA_HEADS * LANES:],
                         preferred_element_type=F32).astype(v_out.dtype)
    gw = _HEAD_GROUP * LANES
    for grp in range(MLA_HEADS // _HEAD_GROUP):
        blk = jnp.dot(ckvn, wukv_ref[:, grp * gw:(grp + 1) * gw], preferred_element_type=F32)
        for hh in range(_HEAD_GROUP):
            kh = blk[:, hh * LANES:(hh + 1) * LANES] + kr
            k_out[grp * _HEAD_GROUP + hh] = ((kh * cos_k + krp) * head_norm(kh)).astype(k_out.dtype)

    lo = lax.broadcasted_iota(jnp.int32, (1, LANES), 1) < DIFF_QK_DIM

    def halfnorm(blk, w):
        sq = blk * blk
        ss_lo = jnp.sum(jnp.where(lo, sq, 0.0), axis=-1, keepdims=True)
        ss_hi = jnp.sum(jnp.where(lo, 0.0, sq), axis=-1, keepdims=True)
        r = jnp.where(lo, lax.rsqrt(ss_lo * (1.0 / DIFF_QK_DIM) + EPS),
                      lax.rsqrt(ss_hi * (1.0 / DIFF_QK_DIM) + EPS))
        return blk * r * w

    dq = seg(_SEG_DQ)
    dqw = dqw_ref[...]
    for h in range(DIFF_HEADS):
        qn = halfnorm(dq[:, h * LANES:(h + 1) * LANES], dqw)
        qd_out[h, 0] = jnp.where(lo, qn, 0.0).astype(qd_out.dtype)
        qd_out[h, 1] = jnp.where(lo, 0.0, qn).astype(qd_out.dtype)
    dk = seg(_SEG_DK)
    dkw = dkw_ref[...]
    for h in range(DIFF_HEADS):
        kd_out[h] = halfnorm(dk[:, h * LANES:(h + 1) * LANES], dkw).astype(kd_out.dtype)
    vd_out[...] = seg(_SEG_DV).astype(vd_out.dtype)

    bg = bg_ref[...]
    ga_out[...] = _sigmoid(seg(_SEG_GA) + bg[:, :D_MODEL]).astype(ga_out.dtype)
    gb_out[...] = _sigmoid(seg(_SEG_GB) + bg[:, D_MODEL:]).astype(gb_out.dtype)


def _proj_call(x2d, b, t, anw, win, qnw, wuq, kvnw, wukv, qhw, khw, dqw, dkw, bg, rc, rs, tm):
    n = b * t
    tpb = t // tm
    row = lambda i: (i, 0)
    head = lambda i: (i // tpb, 0, i % tpb, 0)
    act = lambda: jax.ShapeDtypeStruct((n, D_MODEL), MXU_DTYPE)
    hd = lambda nh: jax.ShapeDtypeStruct((b, nh, t, LANES), MXU_DTYPE)
    return pl.pallas_call(
        _proj_kernel,
        grid=(n // tm,),
        in_specs=[
            pl.BlockSpec((tm, D_MODEL), row),
            _const_spec((1, D_MODEL)),
            _const_spec((D_MODEL, _IN_WIDTH_PACKED)),
            _const_spec((1, MLA_Q_RANK)),
            _const_spec((MLA_Q_RANK, MLA_HEADS * 2 * LANES)),
            _const_spec((1, MLA_KV_RANK)),
            _const_spec((MLA_KV_RANK, MLA_HEADS * LANES + D_MODEL)),
            _const_spec((2, LANES)), _const_spec((2, LANES)),
            _const_spec((1, LANES)), _const_spec((1, LANES)),
            _const_spec((1, 2 * D_MODEL)),
            pl.BlockSpec((tm, LANES), row), pl.BlockSpec((tm, LANES), row),
        ],
        out_specs=[
            pl.BlockSpec((None, MLA_HEADS, tm, LANES), head),
            pl.BlockSpec((None, MLA_HEADS, tm, LANES), head),
            pl.BlockSpec((tm, D_MODEL), row),
            pl.BlockSpec((None, DIFF_HEADS, 2, tm, LANES), lambda i: (i // tpb, 0, 0, i % tpb, 0)),
            pl.BlockSpec((None, DIFF_HEADS, tm, LANES), head),
            pl.BlockSpec((tm, D_MODEL), row),
            pl.BlockSpec((tm, D_MODEL), row),
            pl.BlockSpec((tm, D_MODEL), row),
        ],
        out_shape=[hd(MLA_HEADS), hd(MLA_HEADS), act(),
                   jax.ShapeDtypeStruct((b, DIFF_HEADS, 2, t, LANES), MXU_DTYPE), hd(DIFF_HEADS),
                   act(), act(), act()],
        compiler_params=pltpu.CompilerParams(dimension_semantics=("arbitrary",),
                                             vmem_limit_bytes=VMEM_LIMIT),
        name="proj",
    )(x2d, anw, win, qnw, wuq, kvnw, wukv, qhw, khw, dqw, dkw, bg, rc, rs)


_NT = (((1,), (1,)), ((), ()))


def _fold_rows(x, op):
    parts = [x[i * SUBLANES:(i + 1) * SUBLANES, :] for i in range(x.shape[0] // SUBLANES)]
    return functools.reduce(op, parts)


def _attn_step(g, q, k_ref, v_ref, s_ref, mx_ref, ls_ref, acc_ref, tk, finish, penalty_fn=None, *,
               keys_on_rows):
    nk = k_ref.shape[0] // tk
    key_axis = 0 if keys_on_rows else 1
    fold = _fold_rows if keys_on_rows else _fold_lanes
    all_ = slice(None)
    keys = (lambda off: (pl.ds(off, tk), all_)) if keys_on_rows else (lambda off: (all_, pl.ds(off, tk)))

    @pl.when(g == 0)
    def _():
        s_ref[1][...] = jnp.zeros_like(s_ref[1])
        mx_ref[1][...] = jnp.zeros_like(mx_ref[1])
        acc_ref[0][...] = jnp.zeros_like(acc_ref[0])
        ls_ref[0][...] = jnp.ones_like(ls_ref[0])

    def run(p):
        s_cur, s_prv, mx_cur = s_ref[p], s_ref[1 - p], mx_ref[p]
        ls_prv, acc_prv, ls_old, acc_old = ls_ref[1 - p], acc_ref[1 - p], ls_ref[p], acc_ref[p]
        m_prev = jnp.max(mx_ref[1 - p][...], axis=key_axis, keepdims=True)
        mx_cur[...] = jnp.full(mx_cur.shape, -jnp.inf, F32)
        ls_prv[...] = jnp.zeros_like(ls_prv)
        acc_prv[...] = jnp.zeros_like(acc_prv)

        def body(kc, _):
            off = pl.multiple_of(kc * tk, tk)
            kblk = k_ref[pl.ds(off, tk), :]
            lhs, rhs = (kblk, q) if keys_on_rows else (q, kblk)
            s = lax.dot_general(lhs, rhs, _NT, preferred_element_type=F32)
            if penalty_fn is not None:
                s = s - penalty_fn(off)
            s_cur[keys(off)] = s
            mx_cur[...] = jnp.maximum(mx_cur[...], fold(s, jnp.maximum))

            e = jnp.exp2(s_prv[keys(off)] - m_prev)
            ls_prv[...] += fold(e, jnp.add)
            eb = e.astype(MXU_DTYPE)
            if keys_on_rows:
                acc_prv[...] += jnp.dot(v_ref[:, pl.ds(off, tk)], eb, preferred_element_type=F32)
            else:
                acc_prv[...] += jnp.dot(eb, v_ref[pl.ds(off, tk), :], preferred_element_type=F32)

            def normalised(first, count):
                sl = (all_, pl.ds(first, count)) if keys_on_rows else (pl.ds(first, count), all_)
                o = acc_old[sl] / jnp.sum(ls_old[sl], axis=key_axis, keepdims=True)
                return o.T if keys_on_rows else o

            finish(p, kc, nk, normalised)
            return 0

        lax.fori_loop(0, nk, body, 0, unroll=_pick(nk, (4, 2)))

    for par in range(2):
        @pl.when(g % 2 == par)
        def _():
            run(par)


def _attn_scratch(tq, t, keys_on_rows):
    if keys_on_rows:
        shapes = [(t, tq), (SUBLANES, tq), (SUBLANES, tq), (LANES, tq)]
    else:
        shapes = [(tq, t), (tq, LANES), (tq, LANES), (tq, LANES)]
    return [[pltpu.VMEM(shape, F32)] * 2 for shape in shapes]


def _attn_units(units):
    stage = lambda d: (lambda g: jnp.clip(g - d, 0, units - 1))
    return stage(0), stage(1), stage(2)


def _mla_kernel(q_ref, k_ref, v_ref, o_ref, s_ref, mx_ref, ls_ref, acc_ref, *, tk):
    def finish(p, kc, nk, normalised):
        rows = o_ref.shape[0] // nk
        r0 = pl.multiple_of(kc * rows, rows)
        sl = slice(p * MLA_V_DIM, (p + 1) * MLA_V_DIM)
        o_ref[pl.ds(r0, rows), sl] = normalised(r0, rows)[:, sl].astype(o_ref.dtype)

    _attn_step(pl.program_id(0), q_ref[...], k_ref, v_ref, s_ref, mx_ref, ls_ref, acc_ref, tk, finish,
               keys_on_rows=True)


def _mla_call(q, k, v, tq, tk):
    b, nh, t, _ = q.shape
    nq = t // tq
    units = b * nh * nq
    cur, prev, old = _attn_units(units)

    def unit(u):
        return u // (nh * nq), (u // (2 * nq)) % (nh // 2), (u // 2) % nq, u % 2

    def q_map(g):
        bi, p, i, h = unit(cur(g))
        return bi, 2 * p + h, i, 0

    def k_map(g):
        bi, p, _, h = unit(cur(g))
        return bi, 2 * p + h, 0, 0

    def v_map(g):
        bi, p, _, _ = unit(prev(g))
        return bi, p, 0

    def o_map(g):
        bi, p, i, _ = unit(old(g))
        return bi, i, p

    return pl.pallas_call(
        functools.partial(_mla_kernel, tk=tk),
        grid=(units + 2,),
        in_specs=[
            pl.BlockSpec((None, None, tq, LANES), q_map),
            pl.BlockSpec((None, None, t, LANES), k_map),
            pl.BlockSpec((None, LANES, t), v_map),
        ],
        out_specs=pl.BlockSpec((None, tq, LANES), o_map),
        out_shape=jax.ShapeDtypeStruct((b, t, D_MODEL), MXU_DTYPE),
        scratch_shapes=_attn_scratch(tq, t, True),
        compiler_params=pltpu.CompilerParams(dimension_semantics=("arbitrary",),
                                             vmem_limit_bytes=VMEM_LIMIT),
        name="mla_attn",
    )(q, k, v)


_MAP_GROUP = 16


def _diff_kernel(sc_ref, q_ref, k_ref, v_ref, pq_ref, pk_ref, w_ref, o_ref,
                 s_ref, mx_ref, ls_ref, acc_ref, *, tk, nq, units, out_scale):
    g = pl.program_id(0)
    tq = o_ref.shape[0]
    h_cur = (jnp.minimum(g, units - 1) // nq) % DIFF_HEADS
    slope = sc_ref[h_cur]
    lam = sc_ref[DIFF_HEADS]
    pq = pq_ref[...] * slope
    w = w_ref[...]

    def both_maps(a, b):
        n, c = a.shape
        grouped = lambda x: x.reshape(n // _MAP_GROUP, 1, _MAP_GROUP, c)
        return jnp.concatenate([grouped(a), grouped(b)], axis=1).reshape(2 * n, c)

    def penalty(off):
        pen = jnp.abs(pq - pk_ref[:, pl.ds(off, tk)] * slope)
        return both_maps(pen, pen)

    def finish(p, kc, nk, normalised):
        rows = tq // nk
        r0 = pl.multiple_of(kc * rows, rows)
        o = normalised(2 * r0, 2 * rows).reshape(rows // _MAP_GROUP, 2, _MAP_GROUP, LANES)
        od = (o[:, 0] - lam * o[:, 1]).reshape(rows, LANES)
        o_ref[pl.ds(r0, rows), :] = (_rms(od, w) * out_scale).astype(o_ref.dtype)

    q = both_maps(q_ref[0], q_ref[1])
    _attn_step(g, q, k_ref, v_ref, s_ref, mx_ref, ls_ref, acc_ref, tk, finish, penalty, keys_on_rows=False)


def _diff_call(scalars, qd, k, v, posq, posk, subw, tq, tk, out_scale):
    b, nh, _, t, _ = qd.shape
    nq = t // tq
    units = b * nh * nq
    cur, prev, old = _attn_units(units)

    def unit(u):
        return u // (nh * nq), (u // nq) % nh, u % nq

    def q_map(g):
        bi, h, i = unit(cur(g))
        return bi, h, 0, i, 0

    def k_map(g):
        bi, h, _ = unit(cur(g))
        return bi, h, 0, 0

    def v_map(g):
        bi, h, _ = unit(prev(g))
        return bi, 0, h

    def o_map(g):
        bi, h, i = unit(old(g))
        return bi, i, h

    return pl.pallas_call(
        functools.partial(_diff_kernel, tk=tk, nq=nq, units=units, out_scale=out_scale),
        grid=(units + 2,),
        in_specs=[
            pl.BlockSpec(memory_space=pltpu.SMEM),
            pl.BlockSpec((None, None, 2, tq, LANES), q_map),
            pl.BlockSpec((None, None, t, LANES), k_map),
            pl.BlockSpec((None, t, LANES), v_map),
            pl.BlockSpec((None, tq, 1), lambda g: (unit(cur(g))[0], unit(cur(g))[2], 0)),
            pl.BlockSpec((None, 1, t), lambda g: (unit(cur(g))[0], 0, 0)),
            pl.BlockSpec((1, LANES), lambda g: (0, 0)),
        ],
        out_specs=pl.BlockSpec((None, tq, LANES), o_map),
        out_shape=jax.ShapeDtypeStruct((b, t, D_MODEL), MXU_DTYPE),
        scratch_shapes=_attn_scratch(2 * tq, t, False),
        compiler_params=pltpu.CompilerParams(dimension_semantics=("arbitrary",),
                                             vmem_limit_bytes=VMEM_LIMIT),
        name="diff_attn",
    )(scalars, qd, k, v, posq, posk, subw)


def _split2(a):
    a1 = a.astype(jnp.bfloat16)
    return a1, (a - a1.astype(F32)).astype(jnp.bfloat16)


def _merge_kernel(x_ref, om_ref, od_ref, ga_ref, gb_ref, wo_ref, fnw_ref, rw1_ref, rw2_ref,
                  x1_out, aff_out, h2_out):
    merged = (ga_ref[...].astype(F32) * om_ref[...].astype(F32)
              + gb_ref[...].astype(F32) * od_ref[...].astype(F32))
    x1 = x_ref[...] + jnp.dot(merged.astype(MXU_DTYPE), wo_ref[...], preferred_element_type=F32)
    x1_out[...] = x1
    h2 = _rms(x1, fnw_ref[...])

    h2_out[...] = h2

    h1, hh2 = _split2(h2)
    w1, w2 = rw1_ref[...], rw2_ref[...]
    dot = lambda a, w: jnp.dot(a, w, preferred_element_type=F32)
    logits = (dot(hh2, w1) + dot(h1, w2)) + dot(h1, w1)
    lane = lax.broadcasted_iota(jnp.int32, (1, LANES), 1)
    logits = jnp.where(lane < N_EXPERTS, logits, -jnp.inf)
    e = jnp.exp(logits - jnp.max(logits, axis=-1, keepdims=True))
    aff = e / jnp.sum(e, axis=-1, keepdims=True)
    aff_out[...] = aff[:, :N_EXPERTS]


def _merge_call(x2d, om, od, ga, gb, wo, fnw, rw1, rw2, tm):
    n = x2d.shape[0]
    row = lambda i: (i, 0)
    act = pl.BlockSpec((tm, D_MODEL), row)
    return pl.pallas_call(
        _merge_kernel,
        grid=(n // tm,),
        in_specs=[act, act, act, act, act,
                  _const_spec((D_MODEL, D_MODEL)), _const_spec((1, D_MODEL)),
                  _const_spec((D_MODEL, LANES)), _const_spec((D_MODEL, LANES))],
        out_specs=[act, pl.BlockSpec((tm, N_EXPERTS), row), act],
        out_shape=[jax.ShapeDtypeStruct((n, D_MODEL), F32),
                   jax.ShapeDtypeStruct((n, N_EXPERTS), F32),
                   jax.ShapeDtypeStruct((n, D_MODEL), F32)],
        compiler_params=pltpu.CompilerParams(dimension_semantics=("arbitrary",),
                                             vmem_limit_bytes=VMEM_LIMIT),
        name="merge",
    )(x2d, om, od, ga, gb, wo, fnw, rw1, rw2)


def _route_kernel(a_ref, tri_ref, idx_out, gate_out, rank_ref, *, cap, jc):
    a = a_ref[...]
    ne, t = a.shape

    def search(_, lohi):
        lo, hi = lohi
        mid = lo + ((hi - lo + 1) >> 1)
        cnt = jnp.sum(jnp.where(a >= pltpu.bitcast(mid, F32), 1.0, 0.0), axis=-1, keepdims=True)
        ok = cnt >= cap
        return jnp.where(ok, mid, lo), jnp.where(ok, hi, mid - 1)

    lo0 = jnp.zeros((ne, 1), jnp.int32)
    hi0 = jnp.full((ne, 1), 0x7F800000, jnp.int32)
    thr_bits, _ = lax.fori_loop(0, 32, search, (lo0, hi0))
    thr = pltpu.bitcast(thr_bits, F32)

    tri = tri_ref[...]

    def prefix(mask):
        carry = jnp.zeros((ne, 1), F32)
        cols = []
        for c in range(t // LANES):
            m = mask[:, c * LANES:(c + 1) * LANES]
            cols.append(jnp.dot(m.astype(jnp.bfloat16), tri, preferred_element_type=F32) + carry)
            carry = carry + jnp.sum(m, axis=-1, keepdims=True)
        return jnp.concatenate(cols, axis=1), carry

    gt = jnp.where(a > thr, 1.0, 0.0)
    eq = jnp.where(a == thr, 1.0, 0.0)
    pre_eq, _ = prefix(eq)
    need = cap - jnp.sum(gt, axis=-1, keepdims=True)
    sel = jnp.maximum(gt, jnp.where(pre_eq < need, eq, 0.0))
    rank, _ = prefix(sel)
    rank_ref[...] = jnp.where(sel > 0.0, rank, -1.0)

    tok = lax.broadcasted_iota(jnp.int32, (1, t), 1).astype(F32)

    def per_expert(e, _):
        r = rank_ref[pl.ds(e, 1), :]
        av = a_ref[pl.ds(e, 1), :]

        def per_slots(c, _):
            j0 = pl.multiple_of(c * jc, jc)
            slot = (lax.broadcasted_iota(jnp.int32, (jc, 1), 0) + j0).astype(F32)
            hit = r == slot
            idx_out[e, pl.ds(j0, jc), :] = jnp.sum(jnp.where(hit, tok, 0.0), axis=-1,
                                                   keepdims=True).astype(jnp.int32)
            gate_out[e, pl.ds(j0, jc), :] = jnp.sum(jnp.where(hit, av, 0.0), axis=-1, keepdims=True)
            return 0

        return lax.fori_loop(0, cap // jc, per_slots, 0)

    lax.fori_loop(0, ne, per_expert, 0)


def _route_call(aff_t, tri, cap, jc):
    b, ne, t = aff_t.shape
    return pl.pallas_call(
        functools.partial(_route_kernel, cap=cap, jc=jc),
        grid=(b,),
        in_specs=[pl.BlockSpec((None, ne, t), lambda bi: (bi, 0, 0)), _const_spec((LANES, LANES))],
        out_specs=[pl.BlockSpec((None, ne, cap, 1), lambda bi: (bi, 0, 0, 0)),
                   pl.BlockSpec((None, ne, cap, 1), lambda bi: (bi, 0, 0, 0))],
        out_shape=[jax.ShapeDtypeStruct((b, ne, cap, 1), jnp.int32),
                   jax.ShapeDtypeStruct((b, ne, cap, 1), F32)],
        scratch_shapes=[pltpu.VMEM((ne, t), F32)],
        compiler_params=pltpu.CompilerParams(dimension_semantics=("arbitrary",),
                                             vmem_limit_bytes=VMEM_LIMIT),
        name="route",
    )(aff_t, tri)


_ROW_UNROLL = 8


_FF_CHUNK = 512


def _expert_kernel(idx_ref, idxn_ref, h2_hbm, gate_ref, wg_ref, wu_ref, wd_ref, y_out,
                   xe_ref, hid_ref, sem, *, cap):
    nb = pl.num_programs(1)
    step = pl.program_id(0) * nb + pl.program_id(1)
    last = pl.num_programs(0) * nb - 1
    cur = step % 2
    nxt = 1 - cur
    b_next = jnp.minimum(step + 1, last) % nb

    def row_copy(table_ref, b_src, slot, j):
        return pltpu.make_async_copy(h2_hbm.at[b_src, pl.ds(table_ref[0, j], 1), :],
                                     xe_ref.at[slot, pl.ds(j, 1), :], sem.at[slot])

    def wait_slot(slot):
        pltpu.make_async_copy(xe_ref.at[slot], xe_ref.at[slot], sem.at[slot]).wait()

    @pl.when(step == 0)
    def _():
        def first(j, _):
            row_copy(idx_ref, pl.program_id(1), 0, j).start()
            return 0

        lax.fori_loop(0, cap, first, 0)

    for j in range(cap):
        row_copy(idxn_ref, b_next, nxt, j).start()

    wait_slot(cur)
    xe = xe_ref[cur].astype(MXU_DTYPE)
    for c in range(EXPERT_FF // _FF_CHUNK):
        cols = slice(c * _FF_CHUNK, (c + 1) * _FF_CHUNK)
        g = jnp.dot(xe, wg_ref[:, cols], preferred_element_type=F32)
        u = jnp.dot(xe, wu_ref[:, cols], preferred_element_type=F32)
        hid_ref[:, cols] = (g * _sigmoid(g) * u).astype(MXU_DTYPE)
    y = jnp.dot(hid_ref[...], wd_ref[...], preferred_element_type=F32) * gate_ref[...]
    for c in range(D_MODEL // LANES):
        y_out[pl.ds(c, cap, stride=SUBLANES), :] = y[:, c * LANES:(c + 1) * LANES]

    @pl.when(step == last)
    def _():
        wait_slot(nxt)


def _expert_call(idx_smem, h2, gate, wg, wu, wd, cap):
    b, t, d = h2.shape
    ne = wg.shape[0]

    def next_idx(e, bi):
        s = jnp.minimum(e * b + bi + 1, ne * b - 1)
        return (s % b) * ne + s // b, 0, 0

    return pl.pallas_call(
        functools.partial(_expert_kernel, cap=cap),
        grid=(ne, b),
        in_specs=[
            pl.BlockSpec((None, 1, cap), lambda e, bi: (bi * ne + e, 0, 0), memory_space=pltpu.SMEM),
            pl.BlockSpec((None, 1, cap), next_idx, memory_space=pltpu.SMEM),
            pl.BlockSpec(memory_space=pl.ANY),
            pl.BlockSpec((None, None, cap, 1), lambda e, bi: (bi, e, 0, 0)),
            pl.BlockSpec((None, D_MODEL, EXPERT_FF), lambda e, bi: (e, 0, 0)),
            pl.BlockSpec((None, D_MODEL, EXPERT_FF), lambda e, bi: (e, 0, 0)),
            pl.BlockSpec((None, EXPERT_FF, D_MODEL), lambda e, bi: (e, 0, 0)),
        ],
        out_specs=pl.BlockSpec((None, None, cap * SUBLANES, LANES), lambda e, bi: (bi, e, 0, 0)),
        out_shape=jax.ShapeDtypeStruct((b, ne, cap * SUBLANES, LANES), F32),
        scratch_shapes=[pltpu.VMEM((2, cap, d), F32), pltpu.VMEM((cap, EXPERT_FF), MXU_DTYPE),
                        pltpu.SemaphoreType.DMA((2,))],
        compiler_params=pltpu.CompilerParams(dimension_semantics=("arbitrary",) * 2,
                                             vmem_limit_bytes=VMEM_LIMIT),
        name="experts",
    )(idx_smem, idx_smem, h2, gate, wg, wu, wd)


def _combine_kernel(idx_ref, x1_hbm, y_ref, o_hbm, acc_ref, stage_ref, sem, *, cap):
    bi, e = pl.program_id(0), pl.program_id(1)
    tile = lambda i: pl.ds(pl.multiple_of(i * SUBLANES, SUBLANES), SUBLANES)

    last = e == pl.num_programs(1) - 1
    rows = stage_ref.shape[1]
    n_blocks = acc_ref.shape[0] // (SUBLANES * rows)

    def load(r):
        return pltpu.make_async_copy(x1_hbm.at[bi, pl.ds(r * rows, rows), :], stage_ref.at[r % 2],
                                     sem.at[0, r % 2])

    def store(r):
        return pltpu.make_async_copy(stage_ref.at[r % 2], o_hbm.at[bi, pl.ds(r * rows, rows), :],
                                     sem.at[1, r % 2])

    @pl.when(e == 0)
    def _():
        acc_ref[...] = jnp.zeros_like(acc_ref)

    @pl.when(last)
    def _():
        load(0).start()

    def scatter(g, _):
        toks = [idx_ref[0, g * _ROW_UNROLL + u] for u in range(_ROW_UNROLL)]
        sums = [acc_ref[tile(toks[u]), :] + y_ref[tile(g * _ROW_UNROLL + u), :] for u in range(_ROW_UNROLL)]
        for u in range(_ROW_UNROLL):
            acc_ref[tile(toks[u]), :] = sums[u]
        return 0

    lax.fori_loop(0, cap // _ROW_UNROLL, scatter, 0)

    @pl.when(last)
    def _():
        for r in range(n_blocks):
            if r + 1 < n_blocks:
                if r >= 1:
                    store(r - 1).wait()
                load(r + 1).start()
            load(r).wait()
            for c in range(D_MODEL // LANES):
                stage_ref[r % 2, :, c * LANES:(c + 1) * LANES] += acc_ref[
                    pl.ds(r * rows * SUBLANES + c, rows, stride=SUBLANES), :]
            store(r).start()
        for r in range(max(n_blocks - 2, 0), n_blocks):
            store(r).wait()


def _combine_call(idx_smem, x1, y, cap):
    b, t, d = x1.shape
    ne = y.shape[1]
    return pl.pallas_call(
        functools.partial(_combine_kernel, cap=cap),
        grid=(b, ne),
        in_specs=[
            pl.BlockSpec((None, 1, cap), lambda bi, e: (bi * ne + e, 0, 0), memory_space=pltpu.SMEM),
            pl.BlockSpec(memory_space=pl.ANY),
            pl.BlockSpec((None, None, cap * SUBLANES, LANES), lambda bi, e: (bi, e, 0, 0)),
        ],
        out_specs=pl.BlockSpec(memory_space=pl.ANY),
        out_shape=jax.ShapeDtypeStruct((b, t, d), F32),
        scratch_shapes=[pltpu.VMEM((t * SUBLANES, LANES), F32),
                        pltpu.VMEM((2, _pick(t, (512, 256, 128)), d), F32),
                        pltpu.SemaphoreType.DMA((2, 2))],
        compiler_params=pltpu.CompilerParams(dimension_semantics=("arbitrary",) * 2,
                                             vmem_limit_bytes=VMEM_LIMIT),
        name="combine",
    )(idx_smem, x1, y)


def _pad_heads(w, n_heads, width):
    r = w.shape[0]
    w = w.reshape(r, n_heads, width)
    return jnp.pad(w, ((0, 0), (0, 0), (0, LANES - width))).reshape(r, n_heads * LANES)


def _pack_w_in(w_in):
    o = 0
    parts = {}
    for name, width in (("cq", MLA_Q_RANK), ("ckv", MLA_KV_RANK), ("kr", MLA_ROPE_DIM),
                        ("dq", D_MODEL), ("dk", D_MODEL), ("dv", D_MODEL), ("ga", D_MODEL), ("gb", D_MODEL)):
        parts[name] = w_in[:, o:o + width]
        o += width
    kr = jnp.pad(parts["kr"], ((0, 0), (MLA_NOPE_DIM, LANES - MLA_QK_DIM)))
    return jnp.concatenate([parts["cq"], parts["ckv"], kr, _rope_partner(kr), parts["dq"], parts["dk"],
                            parts["dv"], parts["ga"], parts["gb"]], axis=1).astype(MXU_DTYPE)


def _rope_partner(w):
    half = MLA_ROPE_DIM // 2
    lane = jnp.arange(LANES)
    src = jnp.where((lane >= MLA_NOPE_DIM) & (lane < MLA_NOPE_DIM + half), lane + half,
                    jnp.where((lane >= MLA_NOPE_DIM + half) & (lane < MLA_QK_DIM), lane - half, lane))
    blocks = w.reshape(w.shape[:-1] + (w.shape[-1] // LANES, LANES))
    return jnp.take(blocks, src, axis=-1).reshape(w.shape)


def _rope_tables(positions):
    inv_freq = 1.0 / (ROPE_THETA ** (jnp.arange(0, MLA_ROPE_DIM, 2, dtype=F32) / MLA_ROPE_DIM))
    ang = positions.astype(F32).reshape(-1, 1) * inv_freq
    cos, sin = jnp.cos(ang), jnp.sin(ang)
    n = ang.shape[0]
    z = lambda w: jnp.zeros((n, w), F32)
    tail = LANES - MLA_QK_DIM
    rc = jnp.concatenate([jnp.ones((n, MLA_NOPE_DIM), F32), cos, cos, z(tail)], axis=1)
    rs = jnp.concatenate([z(MLA_NOPE_DIM), -sin, sin, z(tail)], axis=1)
    return rc, rs


def _pick(t, prefs):
    for p in prefs:
        if t % p == 0:
            return p
    return t


def _layer(x, positions, layer_idx, attn_norm_w, w_in, b_gate, mla_q_norm_w, mla_w_uq, mla_kv_norm_w,
           mla_w_ukv, mla_q_hnorm_w, mla_k_hnorm_w, diff_q_hnorm_w, diff_k_hnorm_w, diff_lambda,
           diff_subln_w, w_out, ffn_norm_w, router_w, expert_w_gate, expert_w_up, expert_w_down):
    b, t, d = x.shape
    n = b * t
    cap = CAPACITY_FACTOR * t // N_EXPERTS
    row = lambda v: v.reshape(1, -1).astype(F32)

    win = _pack_w_in(w_in)
    wuq = _pad_heads(mla_w_uq, MLA_HEADS, MLA_QK_DIM)
    wuq = jnp.stack([wuq.reshape(MLA_Q_RANK, MLA_HEADS, LANES),
                     _rope_partner(wuq).reshape(MLA_Q_RANK, MLA_HEADS, LANES)], axis=2)
    wuq = wuq.reshape(MLA_Q_RANK, MLA_HEADS * 2 * LANES).astype(MXU_DTYPE)
    wukv = mla_w_ukv.reshape(MLA_KV_RANK, MLA_HEADS, MLA_NOPE_DIM + MLA_V_DIM)
    wuk = _pad_heads(wukv[:, :, :MLA_NOPE_DIM].reshape(MLA_KV_RANK, -1), MLA_HEADS, MLA_NOPE_DIM)
    wuv = wukv[:, :, MLA_NOPE_DIM:].reshape(MLA_KV_RANK, -1)
    wukv_p = jnp.concatenate([wuk, wuv], axis=1).astype(MXU_DTYPE)
    pad_gain = lambda w: jnp.pad(w.astype(F32), (0, LANES - MLA_QK_DIM)).reshape(1, LANES)
    qhw = pad_gain(mla_q_hnorm_w) * (MLA_QK_DIM ** -0.5 * LOG2E)
    khw = pad_gain(mla_k_hnorm_w)
    qhw = jnp.concatenate([qhw, _rope_partner(qhw)], axis=0)
    khw = jnp.concatenate([khw, _rope_partner(khw)], axis=0)
    dqw = jnp.tile(diff_q_hnorm_w.astype(F32), 2).reshape(1, LANES) * (DIFF_QK_DIM ** -0.5 * LOG2E)
    dkw = jnp.tile(diff_k_hnorm_w.astype(F32), 2).reshape(1, LANES)
    rc, rs = _rope_tables(positions)

    tm = _pick(t, (512, 256, 128))
    q, k, v, qd, kd, vd, ga, gb = _proj_call(
        x.reshape(n, d), b, t, row(attn_norm_w), win, row(mla_q_norm_w), wuq, row(mla_kv_norm_w), wukv_p,
        qhw, khw, dqw, dkw, row(b_gate), rc, rs, tm)

    tq = _pick(t, (1024, 512, 256, 128))
    tk = _pick(t, (512, 256, 128))
    o_mla = _mla_call(q, k, jnp.swapaxes(v.reshape(b, t, d), 1, 2), tq, tk)

    lam_init = 0.8 - 0.6 * math.exp(-0.3 * layer_idx)
    lf = diff_lambda.astype(F32)
    lam = jnp.exp(jnp.sum(lf[0] * lf[1])) - jnp.exp(jnp.sum(lf[2] * lf[3])) + lam_init
    slopes = jnp.asarray([2.0 ** (-8.0 * (i + 1) / DIFF_HEADS) * LOG2E for i in range(DIFF_HEADS)], F32)
    scalars = jnp.concatenate([slopes, lam.reshape(1)])
    posf = positions.astype(F32)
    o_diff = _diff_call(scalars, qd, kd, vd.reshape(b, t, d), posf.reshape(b, t, 1),
                        posf.reshape(b, 1, t), row(diff_subln_w), tq // 2, tk, 1.0 - lam_init)

    rw = jnp.pad(router_w.astype(F32), ((0, 0), (0, LANES - N_EXPERTS)))
    rw1, rw2 = _split2(rw)
    tm2 = _pick(t, (512, 256, 128))
    x1, aff, h2 = _merge_call(x.reshape(n, d), o_mla.reshape(n, d), o_diff.reshape(n, d), ga, gb,
                              w_out.astype(MXU_DTYPE), row(ffn_norm_w), rw1, rw2, tm2)

    tri = (lax.broadcasted_iota(jnp.int32, (LANES, LANES), 0)
           < lax.broadcasted_iota(jnp.int32, (LANES, LANES), 1)).astype(jnp.bfloat16)
    aff_t = jnp.swapaxes(aff.reshape(b, t, N_EXPERTS), 1, 2)
    idx, gate = _route_call(aff_t, tri, cap, _pick(cap, (256, 128, 64, 32, 16, 8)))
    idx_smem = idx.reshape(b * N_EXPERTS, 1, cap)

    y = _expert_call(idx_smem, h2.reshape(b, t, d), gate, expert_w_gate.astype(MXU_DTYPE),
                     expert_w_up.astype(MXU_DTYPE), expert_w_down.astype(MXU_DTYPE), cap)
    return _combine_call(idx_smem, x1.reshape(b, t, d), y, cap)


def kernel(x, positions, attn_norm_w, w_in, b_gate, mla_q_norm_w, mla_w_uq, mla_kv_norm_w, mla_w_ukv,
           mla_q_hnorm_w, mla_k_hnorm_w, diff_q_hnorm_w, diff_k_hnorm_w, diff_lambda, diff_subln_w, w_out,
           ffn_norm_w, router_w, expert_w_gate, expert_w_up, expert_w_down):
    for l in range(attn_norm_w.shape[0]):
        x = _layer(x, positions, l, attn_norm_w[l], w_in[l], b_gate[l], mla_q_norm_w[l], mla_w_uq[l],
                   mla_kv_norm_w[l], mla_w_ukv[l], mla_q_hnorm_w[l], mla_k_hnorm_w[l], diff_q_hnorm_w[l],
                   diff_k_hnorm_w[l], diff_lambda[l], diff_subln_w[l], w_out[l], ffn_norm_w[l], router_w[l],
                   expert_w_gate[l], expert_w_up[l], expert_w_down[l])
    return x
```

```python
import functools
import math

import jax
import jax.numpy as jnp
from jax import lax
from jax.experimental import pallas as pl
from jax.experimental.pallas import tpu as pltpu

F32 = jnp.float32
MXU_DTYPE = jnp.bfloat16

D_MODEL = 1024
MLA_HEADS = 16
MLA_Q_RANK = 256
MLA_KV_RANK = 256
MLA_NOPE_DIM = 64
MLA_ROPE_DIM = 32
MLA_QK_DIM = MLA_NOPE_DIM + MLA_ROPE_DIM
MLA_V_DIM = D_MODEL // MLA_HEADS
ROPE_THETA = 10000.0
DIFF_HEADS = 8
DIFF_QK_DIM = 64
DIFF_V_DIM = D_MODEL // DIFF_HEADS
N_EXPERTS = 16
CAPACITY_FACTOR = 2
EXPERT_FF = 2 * D_MODEL
EPS = 1e-6
LOG2E = math.log2(math.e)

LANES = 128
SUBLANES = 8
VMEM_LIMIT = 56 * 1024 * 1024

_SEG_CQ = (0, 256)
_SEG_CKV = (256, 512)
_SEG_KR = (512, 640)
_SEG_KRR = (640, 768)
_SEG_DQ = (768, 1792)
_SEG_DK = (1792, 2816)
_SEG_DV = (2816, 3840)
_SEG_GA = (3840, 4864)
_SEG_GB = (4864, 5888)
_IN_WIDTH_PACKED = 5888
_HEAD_GROUP = 4


def _rms(x, w):
    return x * lax.rsqrt(jnp.mean(x * x, axis=-1, keepdims=True) + EPS) * w


def _sigmoid(x):
    return 1.0 / (1.0 + jnp.exp(-x))


def _fold_lanes(x, op):
    parts = [x[:, i * LANES:(i + 1) * LANES] for i in range(x.shape[1] // LANES)]
    return functools.reduce(op, parts)


def _const_spec(shape):
    return pl.BlockSpec(shape, lambda *_: (0,) * len(shape), pipeline_mode=pl.Buffered(1))


def _proj_kernel(x_ref, anw_ref, win_ref, qnw_ref, wuq_ref, kvnw_ref, wukv_ref, qhw_ref, khw_ref,
                 dqw_ref, dkw_ref, bg_ref, rc_ref, rs_ref,
                 q_out, k_out, v_out, qd_out, kd_out, vd_out, ga_out, gb_out):
    x = x_ref[...]
    hb = _rms(x, anw_ref[...]).astype(MXU_DTYPE)

    def seg(ab):
        return jnp.dot(hb, win_ref[:, ab[0]:ab[1]], preferred_element_type=F32)

    rc, rs = rc_ref[...], rs_ref[...]
    qhw, khw = qhw_ref[...], khw_ref[...]
    cos_q, sin_q = rc * qhw[0:1], rs * qhw[1:2]
    cos_k, sin_k = rc * khw[0:1], rs * khw[1:2]

    def head_norm(v):
        return lax.rsqrt(jnp.sum(v * v, axis=-1, keepdims=True) * (1.0 / MLA_QK_DIM) + EPS)

    cqn = _rms(seg(_SEG_CQ), qnw_ref[...]).astype(MXU_DTYPE)
    gw = _HEAD_GROUP * 2 * LANES
    for grp in range(MLA_HEADS // _HEAD_GROUP):
        blk = jnp.dot(cqn, wuq_ref[:, grp * gw:(grp + 1) * gw], preferred_element_type=F32)
        for hh in range(_HEAD_GROUP):
            qh = blk[:, (2 * hh) * LANES:(2 * hh + 1) * LANES]
            qp = blk[:, (2 * hh + 1) * LANES:(2 * hh + 2) * LANES]
            q_out[grp * _HEAD_GROUP + hh] = ((qh * cos_q + qp * sin_q) * head_norm(qh)).astype(q_out.dtype)

    ckvn = _rms(seg(_SEG_CKV), kvnw_ref[...]).astype(MXU_DTYPE)
    kr = seg(_SEG_KR)
    krp = seg(_SEG_KRR) * sin_k
    v_out[...] = jnp.dot(ckvn, wukv_ref[:, MLA_HEADS * LANES:],
                         preferred_element_type=F32).astype(v_out.dtype)
    gw = _HEAD_GROUP * LANES
    for grp in range(MLA_HEADS // _HEAD_GROUP):
        blk = jnp.dot(ckvn, wukv_ref[:, grp * gw:(grp + 1) * gw], preferred_element_type=F32)
        for hh in range(_HEAD_GROUP):
            kh = blk[:, hh * LANES:(hh + 1) * LANES] + kr
            k_out[grp * _HEAD_GROUP + hh] = ((kh * cos_k + krp) * head_norm(kh)).astype(k_out.dtype)

    lo = lax.broadcasted_iota(jnp.int32, (1, LANES), 1) < DIFF_QK_DIM

    def halfnorm(blk, w):
        sq = blk * blk
        ss_lo = jnp.sum(jnp.where(lo, sq, 0.0), axis=-1, keepdims=True)
        ss_hi = jnp.sum(jnp.where(lo, 0.0, sq), axis=-1, keepdims=True)
        r = jnp.where(lo, lax.rsqrt(ss_lo * (1.0 / DIFF_QK_DIM) + EPS),
                      lax.rsqrt(ss_hi * (1.0 / DIFF_QK_DIM) + EPS))
        return blk * r * w

    dq = seg(_SEG_DQ)
    dqw = dqw_ref[...]
    for h in range(DIFF_HEADS):
        qn = halfnorm(dq[:, h * LANES:(h + 1) * LANES], dqw)
        qd_out[h, 0] = jnp.where(lo, qn, 0.0).astype(qd_out.dtype)
        qd_out[h, 1] = jnp.where(lo, 0.0, qn).astype(qd_out.dtype)
    dk = seg(_SEG_DK)
    dkw = dkw_ref[...]
    for h in range(DIFF_HEADS):
        kd_out[h] = halfnorm(dk[:, h * LANES:(h + 1) * LANES], dkw).astype(kd_out.dtype)
    vd_out[...] = seg(_SEG_DV).astype(vd_out.dtype)

    bg = bg_ref[...]
    ga_out[...] = _sigmoid(seg(_SEG_GA) + bg[:, :D_MODEL]).astype(ga_out.dtype)
    gb_out[...] = _sigmoid(seg(_SEG_GB) + bg[:, D_MODEL:]).astype(gb_out.dtype)


def _proj_call(x2d, b, t, anw, win, qnw, wuq, kvnw, wukv, qhw, khw, dqw, dkw, bg, rc, rs, tm):
    n = b * t
    tpb = t // tm
    row = lambda i: (i, 0)
    head = lambda i: (i // tpb, 0, i % tpb, 0)
    act = lambda: jax.ShapeDtypeStruct((n, D_MODEL), MXU_DTYPE)
    hd = lambda nh: jax.ShapeDtypeStruct((b, nh, t, LANES), MXU_DTYPE)
    return pl.pallas_call(
        _proj_kernel,
        grid=(n // tm,),
        in_specs=[
            pl.BlockSpec((tm, D_MODEL), row),
            _const_spec((1, D_MODEL)),
            _const_spec((D_MODEL, _IN_WIDTH_PACKED)),
            _const_spec((1, MLA_Q_RANK)),
            _const_spec((MLA_Q_RANK, MLA_HEADS * 2 * LANES)),
            _const_spec((1, MLA_KV_RANK)),
            _const_spec((MLA_KV_RANK, MLA_HEADS * LANES + D_MODEL)),
            _const_spec((2, LANES)), _const_spec((2, LANES)),
            _const_spec((1, LANES)), _const_spec((1, LANES)),
            _const_spec((1, 2 * D_MODEL)),
            pl.BlockSpec((tm, LANES), row), pl.BlockSpec((tm, LANES), row),
        ],
        out_specs=[
            pl.BlockSpec((None, MLA_HEADS, tm, LANES), head),
            pl.BlockSpec((None, MLA_HEADS, tm, LANES), head),
            pl.BlockSpec((tm, D_MODEL), row),
            pl.BlockSpec((None, DIFF_HEADS, 2, tm, LANES), lambda i: (i // tpb, 0, 0, i % tpb, 0)),
            pl.BlockSpec((None, DIFF_HEADS, tm, LANES), head),
            pl.BlockSpec((tm, D_MODEL), row),
            pl.BlockSpec((tm, D_MODEL), row),
            pl.BlockSpec((tm, D_MODEL), row),
        ],
        out_shape=[hd(MLA_HEADS), hd(MLA_HEADS), act(),
                   jax.ShapeDtypeStruct((b, DIFF_HEADS, 2, t, LANES), MXU_DTYPE), hd(DIFF_HEADS),
                   act(), act(), act()],
        compiler_params=pltpu.CompilerParams(dimension_semantics=("arbitrary",),
                                             vmem_limit_bytes=VMEM_LIMIT),
        name="proj",
    )(x2d, anw, win, qnw, wuq, kvnw, wukv, qhw, khw, dqw, dkw, bg, rc, rs)


_NT = (((1,), (1,)), ((), ()))


def _fold_rows(x, op):
    parts = [x[i * SUBLANES:(i + 1) * SUBLANES, :] for i in range(x.shape[0] // SUBLANES)]
    return functools.reduce(op, parts)


def _attn_step(g, q, k_ref, v_ref, s_ref, mx_ref, ls_ref, acc_ref, tk, finish, penalty_fn=None, *,
               keys_on_rows):
    nk = k_ref.shape[0] // tk
    key_axis = 0 if keys_on_rows else 1
    fold = _fold_rows if keys_on_rows else _fold_lanes
    all_ = slice(None)
    keys = (lambda off: (pl.ds(off, tk), all_)) if keys_on_rows else (lambda off: (all_, pl.ds(off, tk)))

    @pl.when(g == 0)
    def _():
        s_ref[1][...] = jnp.zeros_like(s_ref[1])
        mx_ref[1][...] = jnp.zeros_like(mx_ref[1])
        acc_ref[0][...] = jnp.zeros_like(acc_ref[0])
        ls_ref[0][...] = jnp.ones_like(ls_ref[0])

    def run(p):
        s_cur, s_prv, mx_cur = s_ref[p], s_ref[1 - p], mx_ref[p]
        ls_prv, acc_prv, ls_old, acc_old = ls_ref[1 - p], acc_ref[1 - p], ls_ref[p], acc_ref[p]
        m_prev = jnp.max(mx_ref[1 - p][...], axis=key_axis, keepdims=True)
        mx_cur[...] = jnp.full(mx_cur.shape, -jnp.inf, F32)
        ls_prv[...] = jnp.zeros_like(ls_prv)
        acc_prv[...] = jnp.zeros_like(acc_prv)

        def body(kc, _):
            off = pl.multiple_of(kc * tk, tk)
            kblk = k_ref[pl.ds(off, tk), :]
            lhs, rhs = (kblk, q) if keys_on_rows else (q, kblk)
            s = lax.dot_general(lhs, rhs, _NT, preferred_element_type=F32)
            if penalty_fn is not None:
                s = s - penalty_fn(off)
            s_cur[keys(off)] = s
            mx_cur[...] = jnp.maximum(mx_cur[...], fold(s, jnp.maximum))

            e = jnp.exp2(s_prv[keys(off)] - m_prev)
            ls_prv[...] += fold(e, jnp.add)
            eb = e.astype(MXU_DTYPE)
            if keys_on_rows:
                acc_prv[...] += jnp.dot(v_ref[:, pl.ds(off, tk)], eb, preferred_element_type=F32)
            else:
                acc_prv[...] += jnp.dot(eb, v_ref[pl.ds(off, tk), :], preferred_element_type=F32)

            def normalised(first, count):
                sl = (all_, pl.ds(first, count)) if keys_on_rows else (pl.ds(first, count), all_)
                o = acc_old[sl] / jnp.sum(ls_old[sl], axis=key_axis, keepdims=True)
                return o.T if keys_on_rows else o

            finish(p, kc, nk, normalised)
            return 0

        lax.fori_loop(0, nk, body, 0, unroll=_pick(nk, (4, 2)))

    for par in range(2):
        @pl.when(g % 2 == par)
        def _():
            run(par)


def _attn_scratch(tq, t, keys_on_rows):
    if keys_on_rows:
        shapes = [(t, tq), (SUBLANES, tq), (SUBLANES, tq), (LANES, tq)]
    else:
        shapes = [(tq, t), (tq, LANES), (tq, LANES), (tq, LANES)]
    return [[pltpu.VMEM(shape, F32)] * 2 for shape in shapes]


def _attn_units(units):
    stage = lambda d: (lambda g: jnp.clip(g - d, 0, units - 1))
    return stage(0), stage(1), stage(2)


def _mla_kernel(q_ref, k_ref, v_ref, o_ref, s_ref, mx_ref, ls_ref, acc_ref, *, tk):
    def finish(p, kc, nk, normalised):
        rows = o_ref.shape[0] // nk
        r0 = pl.multiple_of(kc * rows, rows)
        sl = slice(p * MLA_V_DIM, (p + 1) * MLA_V_DIM)
        o_ref[pl.ds(r0, rows), sl] = normalised(r0, rows)[:, sl].astype(o_ref.dtype)

    _attn_step(pl.program_id(0), q_ref[...], k_ref, v_ref, s_ref, mx_ref, ls_ref, acc_ref, tk, finish,
               keys_on_rows=True)


def _mla_call(q, k, v, tq, tk):
    b, nh, t, _ = q.shape
    nq = t // tq
    units = b * nh * nq
    cur, prev, old = _attn_units(units)

    def unit(u):
        return u // (nh * nq), (u // (2 * nq)) % (nh // 2), (u // 2) % nq, u % 2

    def q_map(g):
        bi, p, i, h = unit(cur(g))
        return bi, 2 * p + h, i, 0

    def k_map(g):
        bi, p, _, h = unit(cur(g))
        return bi, 2 * p + h, 0, 0

    def v_map(g):
        bi, p, _, _ = unit(prev(g))
        return bi, p, 0

    def o_map(g):
        bi, p, i, _ = unit(old(g))
        return bi, i, p

    return pl.pallas_call(
        functools.partial(_mla_kernel, tk=tk),
        grid=(units + 2,),
        in_specs=[
            pl.BlockSpec((None, None, tq, LANES), q_map),
            pl.BlockSpec((None, None, t, LANES), k_map),
            pl.BlockSpec((None, LANES, t), v_map),
        ],
        out_specs=pl.BlockSpec((None, tq, LANES), o_map),
        out_shape=jax.ShapeDtypeStruct((b, t, D_MODEL), MXU_DTYPE),
        scratch_shapes=_attn_scratch(tq, t, True),
        compiler_params=pltpu.CompilerParams(dimension_semantics=("arbitrary",),
                                             vmem_limit_bytes=VMEM_LIMIT),
        name="mla_attn",
    )(q, k, v)


_MAP_GROUP = 16


def _diff_kernel(sc_ref, q_ref, k_ref, v_ref, pq_ref, pk_ref, w_ref, o_ref,
                 s_ref, mx_ref, ls_ref, acc_ref, *, tk, nq, units, out_scale):
    g = pl.program_id(0)
    tq = o_ref.shape[0]
    h_cur = (jnp.minimum(g, units - 1) // nq) % DIFF_HEADS
    slope = sc_ref[h_cur]
    lam = sc_ref[DIFF_HEADS]
    pq = pq_ref[...] * slope
    w = w_ref[...]

    def both_maps(a, b):
        n, c = a.shape
        grouped = lambda x: x.reshape(n // _MAP_GROUP, 1, _MAP_GROUP, c)
        return jnp.concatenate([grouped(a), grouped(b)], axis=1).reshape(2 * n, c)

    def penalty(off):
        pen = jnp.abs(pq - pk_ref[:, pl.ds(off, tk)] * slope)
        return both_maps(pen, pen)

    def finish(p, kc, nk, normalised):
        rows = tq // nk
        r0 = pl.multiple_of(kc * rows, rows)
        o = normalised(2 * r0, 2 * rows).reshape(rows // _MAP_GROUP, 2, _MAP_GROUP, LANES)
        od = (o[:, 0] - lam * o[:, 1]).reshape(rows, LANES)
        o_ref[pl.ds(r0, rows), :] = (_rms(od, w) * out_scale).astype(o_ref.dtype)

    q = both_maps(q_ref[0], q_ref[1])
    _attn_step(g, q, k_ref, v_ref, s_ref, mx_ref, ls_ref, acc_ref, tk, finish, penalty, keys_on_rows=False)


def _diff_call(scalars, qd, k, v, posq, posk, subw, tq, tk, out_scale):
    b, nh, _, t, _ = qd.shape
    nq = t // tq
    units = b * nh * nq
    cur, prev, old = _attn_units(units)

    def unit(u):
        return u // (nh * nq), (u // nq) % nh, u % nq

    def q_map(g):
        bi, h, i = unit(cur(g))
        return bi, h, 0, i, 0

    def k_map(g):
        bi, h, _ = unit(cur(g))
        return bi, h, 0, 0

    def v_map(g):
        bi, h, _ = unit(prev(g))
        return bi, 0, h

    def o_map(g):
        bi, h, i = unit(old(g))
        return bi, i, h

    return pl.pallas_call(
        functools.partial(_diff_kernel, tk=tk, nq=nq, units=units, out_scale=out_scale),
        grid=(units + 2,),
        in_specs=[
            pl.BlockSpec(memory_space=pltpu.SMEM),
            pl.BlockSpec((None, None, 2, tq, LANES), q_map),
            pl.BlockSpec((None, None, t, LANES), k_map),
            pl.BlockSpec((None, t, LANES), v_map),
            pl.BlockSpec((None, tq, 1), lambda g: (unit(cur(g))[0], unit(cur(g))[2], 0)),
            pl.BlockSpec((None, 1, t), lambda g: (unit(cur(g))[0], 0, 0)),
            pl.BlockSpec((1, LANES), lambda g: (0, 0)),
        ],
        out_specs=pl.BlockSpec((None, tq, LANES), o_map),
        out_shape=jax.ShapeDtypeStruct((b, t, D_MODEL), MXU_DTYPE),
        scratch_shapes=_attn_scratch(2 * tq, t, False),
        compiler_params=pltpu.CompilerParams(dimension_semantics=("arbitrary",),
                                             vmem_limit_bytes=VMEM_LIMIT),
        name="diff_attn",
    )(scalars, qd, k, v, posq, posk, subw)


def _split2(a):
    a1 = a.astype(jnp.bfloat16)
    return a1, (a - a1.astype(F32)).astype(jnp.bfloat16)


def _merge_kernel(x_ref, om_ref, od_ref, ga_ref, gb_ref, wo_ref, fnw_ref, rw1_ref, rw2_ref,
                  x1_out, aff_out, h2_out):
    merged = (ga_ref[...].astype(F32) * om_ref[...].astype(F32)
              + gb_ref[...].astype(F32) * od_ref[...].astype(F32))
    x1 = x_ref[...] + jnp.dot(merged.astype(MXU_DTYPE), wo_ref[...], preferred_element_type=F32)
    x1_out[...] = x1
    h2 = _rms(x1, fnw_ref[...])

    for c in range(D_MODEL // LANES):
        h2_out[pl.ds(c, h2.shape[0], stride=SUBLANES), :] = h2[:, c * LANES:(c + 1) * LANES]

    h1, hh2 = _split2(h2)
    w1, w2 = rw1_ref[...], rw2_ref[...]
    dot = lambda a, w: jnp.dot(a, w, preferred_element_type=F32)
    logits = (dot(hh2, w1) + dot(h1, w2)) + dot(h1, w1)
    lane = lax.broadcasted_iota(jnp.int32, (1, LANES), 1)
    logits = jnp.where(lane < N_EXPERTS, logits, -jnp.inf)
    e = jnp.exp(logits - jnp.max(logits, axis=-1, keepdims=True))
    aff = e / jnp.sum(e, axis=-1, keepdims=True)
    aff_out[...] = aff[:, :N_EXPERTS]


def _merge_call(x2d, om, od, ga, gb, wo, fnw, rw1, rw2, tm):
    n = x2d.shape[0]
    row = lambda i: (i, 0)
    act = pl.BlockSpec((tm, D_MODEL), row)
    return pl.pallas_call(
        _merge_kernel,
        grid=(n // tm,),
        in_specs=[act, act, act, act, act,
                  _const_spec((D_MODEL, D_MODEL)), _const_spec((1, D_MODEL)),
                  _const_spec((D_MODEL, LANES)), _const_spec((D_MODEL, LANES))],
        out_specs=[act, pl.BlockSpec((tm, N_EXPERTS), row), pl.BlockSpec((tm * SUBLANES, LANES), row)],
        out_shape=[jax.ShapeDtypeStruct((n, D_MODEL), F32),
                   jax.ShapeDtypeStruct((n, N_EXPERTS), F32),
                   jax.ShapeDtypeStruct((n * SUBLANES, LANES), F32)],
        compiler_params=pltpu.CompilerParams(dimension_semantics=("arbitrary",),
                                             vmem_limit_bytes=VMEM_LIMIT),
        name="merge",
    )(x2d, om, od, ga, gb, wo, fnw, rw1, rw2)


def _route_kernel(a_ref, tri_ref, idx_out, gate_out, rank_ref, *, cap, jc):
    a = a_ref[...]
    ne, t = a.shape

    def search(_, lohi):
        lo, hi = lohi
        mid = lo + ((hi - lo + 1) >> 1)
        cnt = jnp.sum(jnp.where(a >= pltpu.bitcast(mid, F32), 1.0, 0.0), axis=-1, keepdims=True)
        ok = cnt >= cap
        return jnp.where(ok, mid, lo), jnp.where(ok, hi, mid - 1)

    lo0 = jnp.zeros((ne, 1), jnp.int32)
    hi0 = jnp.full((ne, 1), 0x7F800000, jnp.int32)
    thr_bits, _ = lax.fori_loop(0, 32, search, (lo0, hi0))
    thr = pltpu.bitcast(thr_bits, F32)

    tri = tri_ref[...]

    def prefix(mask):
        carry = jnp.zeros((ne, 1), F32)
        cols = []
        for c in range(t // LANES):
            m = mask[:, c * LANES:(c + 1) * LANES]
            cols.append(jnp.dot(m.astype(jnp.bfloat16), tri, preferred_element_type=F32) + carry)
            carry = carry + jnp.sum(m, axis=-1, keepdims=True)
        return jnp.concatenate(cols, axis=1), carry

    gt = jnp.where(a > thr, 1.0, 0.0)
    eq = jnp.where(a == thr, 1.0, 0.0)
    pre_eq, _ = prefix(eq)
    need = cap - jnp.sum(gt, axis=-1, keepdims=True)
    sel = jnp.maximum(gt, jnp.where(pre_eq < need, eq, 0.0))
    rank, _ = prefix(sel)
    rank_ref[...] = jnp.where(sel > 0.0, rank, -1.0)

    tok = lax.broadcasted_iota(jnp.int32, (1, t), 1).astype(F32)

    def per_expert(e, _):
        r = rank_ref[pl.ds(e, 1), :]
        av = a_ref[pl.ds(e, 1), :]

        def per_slots(c, _):
            j0 = pl.multiple_of(c * jc, jc)
            slot = (lax.broadcasted_iota(jnp.int32, (jc, 1), 0) + j0).astype(F32)
            hit = r == slot
            idx_out[e, pl.ds(j0, jc), :] = jnp.sum(jnp.where(hit, tok, 0.0), axis=-1,
                                                   keepdims=True).astype(jnp.int32)
            gate_out[e, pl.ds(j0, jc), :] = jnp.sum(jnp.where(hit, av, 0.0), axis=-1, keepdims=True)
            return 0

        return lax.fori_loop(0, cap // jc, per_slots, 0)

    lax.fori_loop(0, ne, per_expert, 0)


def _route_call(aff_t, tri, cap, jc):
    b, ne, t = aff_t.shape
    return pl.pallas_call(
        functools.partial(_route_kernel, cap=cap, jc=jc),
        grid=(b,),
        in_specs=[pl.BlockSpec((None, ne, t), lambda bi: (bi, 0, 0)), _const_spec((LANES, LANES))],
        out_specs=[pl.BlockSpec((None, ne, cap, 1), lambda bi: (bi, 0, 0, 0)),
                   pl.BlockSpec((None, ne, cap, 1), lambda bi: (bi, 0, 0, 0))],
        out_shape=[jax.ShapeDtypeStruct((b, ne, cap, 1), jnp.int32),
                   jax.ShapeDtypeStruct((b, ne, cap, 1), F32)],
        scratch_shapes=[pltpu.VMEM((ne, t), F32)],
        compiler_params=pltpu.CompilerParams(dimension_semantics=("arbitrary",),
                                             vmem_limit_bytes=VMEM_LIMIT),
        name="route",
    )(aff_t, tri)


_ROW_UNROLL = 8


_FF_CHUNK = 512


def _expert_kernel(idx_ref, idxn_ref, h2_hbm, gate_ref, wg_ref, wu_ref, wd_ref, y_out,
                   xe_ref, hid_ref, sem, *, cap, seq):
    nb = pl.num_programs(1)
    step = pl.program_id(0) * nb + pl.program_id(1)
    last = pl.num_programs(0) * nb - 1
    cur = step % 2
    nxt = 1 - cur
    b_next = jnp.minimum(step + 1, last) % nb
    tile = lambda i: pl.ds(pl.multiple_of(i * SUBLANES, SUBLANES), SUBLANES)

    def row_copy(table_ref, b_src, slot, j):
        return pltpu.make_async_copy(h2_hbm.at[tile(b_src * seq + table_ref[0, j]), :],
                                     xe_ref.at[slot, tile(j), :], sem.at[slot])

    def wait_slot(slot):
        pltpu.make_async_copy(xe_ref.at[slot], xe_ref.at[slot], sem.at[slot]).wait()

    @pl.when(step == 0)
    def _():
        def first(j, _):
            row_copy(idx_ref, pl.program_id(1), 0, j).start()
            return 0

        lax.fori_loop(0, cap, first, 0)

    for j in range(cap):
        row_copy(idxn_ref, b_next, nxt, j).start()

    wait_slot(cur)
    xe = jnp.concatenate([xe_ref[cur, pl.ds(c, cap, stride=SUBLANES), :] for c in range(D_MODEL // LANES)],
                         axis=1).astype(MXU_DTYPE)
    for c in range(EXPERT_FF // _FF_CHUNK):
        cols = slice(c * _FF_CHUNK, (c + 1) * _FF_CHUNK)
        g = jnp.dot(xe, wg_ref[:, cols], preferred_element_type=F32)
        u = jnp.dot(xe, wu_ref[:, cols], preferred_element_type=F32)
        hid_ref[:, cols] = (g * _sigmoid(g) * u).astype(MXU_DTYPE)
    y = jnp.dot(hid_ref[...], wd_ref[...], preferred_element_type=F32) * gate_ref[...]
    for c in range(D_MODEL // LANES):
        y_out[pl.ds(c, cap, stride=SUBLANES), :] = y[:, c * LANES:(c + 1) * LANES]

    @pl.when(step == last)
    def _():
        wait_slot(nxt)


def _expert_call(idx_smem, h2, b, gate, wg, wu, wd, cap):
    t = h2.shape[0] // (b * SUBLANES)
    ne = wg.shape[0]

    def next_idx(e, bi):
        s = jnp.minimum(e * b + bi + 1, ne * b - 1)
        return (s % b) * ne + s // b, 0, 0

    return pl.pallas_call(
        functools.partial(_expert_kernel, cap=cap, seq=t),
        grid=(ne, b),
        in_specs=[
            pl.BlockSpec((None, 1, cap), lambda e, bi: (bi * ne + e, 0, 0), memory_space=pltpu.SMEM),
            pl.BlockSpec((None, 1, cap), next_idx, memory_space=pltpu.SMEM),
            pl.BlockSpec(memory_space=pl.ANY),
            pl.BlockSpec((None, None, cap, 1), lambda e, bi: (bi, e, 0, 0)),
            pl.BlockSpec((None, D_MODEL, EXPERT_FF), lambda e, bi: (e, 0, 0)),
            pl.BlockSpec((None, D_MODEL, EXPERT_FF), lambda e, bi: (e, 0, 0)),
            pl.BlockSpec((None, EXPERT_FF, D_MODEL), lambda e, bi: (e, 0, 0)),
        ],
        out_specs=pl.BlockSpec((None, None, cap * SUBLANES, LANES), lambda e, bi: (bi, e, 0, 0)),
        out_shape=jax.ShapeDtypeStruct((b, ne, cap * SUBLANES, LANES), F32),
        scratch_shapes=[pltpu.VMEM((2, cap * SUBLANES, LANES), F32), pltpu.VMEM((cap, EXPERT_FF), MXU_DTYPE),
                        pltpu.SemaphoreType.DMA((2,))],
        compiler_params=pltpu.CompilerParams(dimension_semantics=("arbitrary",) * 2,
                                             vmem_limit_bytes=VMEM_LIMIT),
        name="experts",
    )(idx_smem, idx_smem, h2, gate, wg, wu, wd)


def _combine_kernel(idx_ref, x1_hbm, y_ref, o_hbm, acc_ref, stage_ref, sem, *, cap):
    bi, e = pl.program_id(0), pl.program_id(1)
    tile = lambda i: pl.ds(pl.multiple_of(i * SUBLANES, SUBLANES), SUBLANES)

    last = e == pl.num_programs(1) - 1
    rows = stage_ref.shape[1]
    n_blocks = acc_ref.shape[0] // (SUBLANES * rows)

    def load(r):
        return pltpu.make_async_copy(x1_hbm.at[bi, pl.ds(r * rows, rows), :], stage_ref.at[r % 2],
                                     sem.at[0, r % 2])

    def store(r):
        return pltpu.make_async_copy(stage_ref.at[r % 2], o_hbm.at[bi, pl.ds(r * rows, rows), :],
                                     sem.at[1, r % 2])

    @pl.when(e == 0)
    def _():
        acc_ref[...] = jnp.zeros_like(acc_ref)

    @pl.when(last)
    def _():
        load(0).start()

    def scatter(g, _):
        toks = [idx_ref[0, g * _ROW_UNROLL + u] for u in range(_ROW_UNROLL)]
        sums = [acc_ref[tile(toks[u]), :] + y_ref[tile(g * _ROW_UNROLL + u), :] for u in range(_ROW_UNROLL)]
        for u in range(_ROW_UNROLL):
            acc_ref[tile(toks[u]), :] = sums[u]
        return 0

    lax.fori_loop(0, cap // _ROW_UNROLL, scatter, 0)

    @pl.when(last)
    def _():
        for r in range(n_blocks):
            if r + 1 < n_blocks:
                if r >= 1:
                    store(r - 1).wait()
                load(r + 1).start()
            load(r).wait()
            for c in range(D_MODEL // LANES):
                stage_ref[r % 2, :, c * LANES:(c + 1) * LANES] += acc_ref[
                    pl.ds(r * rows * SUBLANES + c, rows, stride=SUBLANES), :]
            store(r).start()
        for r in range(max(n_blocks - 2, 0), n_blocks):
            store(r).wait()


def _combine_call(idx_smem, x1, y, cap):
    b, t, d = x1.shape
    ne = y.shape[1]
    return pl.pallas_call(
        functools.partial(_combine_kernel, cap=cap),
        grid=(b, ne),
        in_specs=[
            pl.BlockSpec((None, 1, cap), lambda bi, e: (bi * ne + e, 0, 0), memory_space=pltpu.SMEM),
            pl.BlockSpec(memory_space=pl.ANY),
            pl.BlockSpec((None, None, cap * SUBLANES, LANES), lambda bi, e: (bi, e, 0, 0)),
        ],
        out_specs=pl.BlockSpec(memory_space=pl.ANY),
        out_shape=jax.ShapeDtypeStruct((b, t, d), F32),
        scratch_shapes=[pltpu.VMEM((t * SUBLANES, LANES), F32),
                        pltpu.VMEM((2, _pick(t, (512, 256, 128)), d), F32),
                        pltpu.SemaphoreType.DMA((2, 2))],
        compiler_params=pltpu.CompilerParams(dimension_semantics=("arbitrary",) * 2,
                                             vmem_limit_bytes=VMEM_LIMIT),
        name="combine",
    )(idx_smem, x1, y)


def _pad_heads(w, n_heads, width):
    r = w.shape[0]
    w = w.reshape(r, n_heads, width)
    return jnp.pad(w, ((0, 0), (0, 0), (0, LANES - width))).reshape(r, n_heads * LANES)


def _pack_w_in(w_in):
    o = 0
    parts = {}
    for name, width in (("cq", MLA_Q_RANK), ("ckv", MLA_KV_RANK), ("kr", MLA_ROPE_DIM),
                        ("dq", D_MODEL), ("dk", D_MODEL), ("dv", D_MODEL), ("ga", D_MODEL), ("gb", D_MODEL)):
        parts[name] = w_in[:, o:o + width]
        o += width
    kr = jnp.pad(parts["kr"], ((0, 0), (MLA_NOPE_DIM, LANES - MLA_QK_DIM)))
    return jnp.concatenate([parts["cq"], parts["ckv"], kr, _rope_partner(kr), parts["dq"], parts["dk"],
                            parts["dv"], parts["ga"], parts["gb"]], axis=1).astype(MXU_DTYPE)


def _rope_partner(w):
    half = MLA_ROPE_DIM // 2
    lane = jnp.arange(LANES)
    src = jnp.where((lane >= MLA_NOPE_DIM) & (lane < MLA_NOPE_DIM + half), lane + half,
                    jnp.where((lane >= MLA_NOPE_DIM + half) & (lane < MLA_QK_DIM), lane - half, lane))
    blocks = w.reshape(w.shape[:-1] + (w.shape[-1] // LANES, LANES))
    return jnp.take(blocks, src, axis=-1).reshape(w.shape)


def _rope_tables(positions):
    inv_freq = 1.0 / (ROPE_THETA ** (jnp.arange(0, MLA_ROPE_DIM, 2, dtype=F32) / MLA_ROPE_DIM))
    ang = positions.astype(F32).reshape(-1, 1) * inv_freq
    cos, sin = jnp.cos(ang), jnp.sin(ang)
    n = ang.shape[0]
    z = lambda w: jnp.zeros((n, w), F32)
    tail = LANES - MLA_QK_DIM
    rc = jnp.concatenate([jnp.ones((n, MLA_NOPE_DIM), F32), cos, cos, z(tail)], axis=1)
    rs = jnp.concatenate([z(MLA_NOPE_DIM), -sin, sin, z(tail)], axis=1)
    return rc, rs


def _pick(t, prefs):
    for p in prefs:
        if t % p == 0:
            return p
    return t


def _layer(x, positions, layer_idx, attn_norm_w, w_in, b_gate, mla_q_norm_w, mla_w_uq, mla_kv_norm_w,
           mla_w_ukv, mla_q_hnorm_w, mla_k_hnorm_w, diff_q_hnorm_w, diff_k_hnorm_w, diff_lambda,
           diff_subln_w, w_out, ffn_norm_w, router_w, expert_w_gate, expert_w_up, expert_w_down):
    b, t, d = x.shape
    n = b * t
    cap = CAPACITY_FACTOR * t // N_EXPERTS
    row = lambda v: v.reshape(1, -1).astype(F32)

    win = _pack_w_in(w_in)
    wuq = _pad_heads(mla_w_uq, MLA_HEADS, MLA_QK_DIM)
    wuq = jnp.stack([wuq.reshape(MLA_Q_RANK, MLA_HEADS, LANES),
                     _rope_partner(wuq).reshape(MLA_Q_RANK, MLA_HEADS, LANES)], axis=2)
    wuq = wuq.reshape(MLA_Q_RANK, MLA_HEADS * 2 * LANES).astype(MXU_DTYPE)
    wukv = mla_w_ukv.reshape(MLA_KV_RANK, MLA_HEADS, MLA_NOPE_DIM + MLA_V_DIM)
    wuk = _pad_heads(wukv[:, :, :MLA_NOPE_DIM].reshape(MLA_KV_RANK, -1), MLA_HEADS, MLA_NOPE_DIM)
    wuv = wukv[:, :, MLA_NOPE_DIM:].reshape(MLA_KV_RANK, -1)
    wukv_p = jnp.concatenate([wuk, wuv], axis=1).astype(MXU_DTYPE)
    pad_gain = lambda w: jnp.pad(w.astype(F32), (0, LANES - MLA_QK_DIM)).reshape(1, LANES)
    qhw = pad_gain(mla_q_hnorm_w) * (MLA_QK_DIM ** -0.5 * LOG2E)
    khw = pad_gain(mla_k_hnorm_w)
    qhw = jnp.concatenate([qhw, _rope_partner(qhw)], axis=0)
    khw = jnp.concatenate([khw, _rope_partner(khw)], axis=0)
    dqw = jnp.tile(diff_q_hnorm_w.astype(F32), 2).reshape(1, LANES) * (DIFF_QK_DIM ** -0.5 * LOG2E)
    dkw = jnp.tile(diff_k_hnorm_w.astype(F32), 2).reshape(1, LANES)
    rc, rs = _rope_tables(positions)

    tm = _pick(t, (512, 256, 128))
    q, k, v, qd, kd, vd, ga, gb = _proj_call(
        x.reshape(n, d), b, t, row(attn_norm_w), win, row(mla_q_norm_w), wuq, row(mla_kv_norm_w), wukv_p,
        qhw, khw, dqw, dkw, row(b_gate), rc, rs, tm)

    tq = _pick(t, (1024, 512, 256, 128))
    tk = _pick(t, (512, 256, 128))
    o_mla = _mla_call(q, k, jnp.swapaxes(v.reshape(b, t, d), 1, 2), tq, tk)

    lam_init = 0.8 - 0.6 * math.exp(-0.3 * layer_idx)
    lf = diff_lambda.astype(F32)
    lam = jnp.exp(jnp.sum(lf[0] * lf[1])) - jnp.exp(jnp.sum(lf[2] * lf[3])) + lam_init
    slopes = jnp.asarray([2.0 ** (-8.0 * (i + 1) / DIFF_HEADS) * LOG2E for i in range(DIFF_HEADS)], F32)
    scalars = jnp.concatenate([slopes, lam.reshape(1)])
    posf = positions.astype(F32)
    o_diff = _diff_call(scalars, qd, kd, vd.reshape(b, t, d), posf.reshape(b, t, 1),
                        posf.reshape(b, 1, t), row(diff_subln_w), tq // 2, tk, 1.0 - lam_init)

    rw = jnp.pad(router_w.astype(F32), ((0, 0), (0, LANES - N_EXPERTS)))
    rw1, rw2 = _split2(rw)
    tm2 = _pick(t, (512, 256, 128))
    x1, aff, h2 = _merge_call(x.reshape(n, d), o_mla.reshape(n, d), o_diff.reshape(n, d), ga, gb,
                              w_out.astype(MXU_DTYPE), row(ffn_norm_w), rw1, rw2, tm2)

    tri = (lax.broadcasted_iota(jnp.int32, (LANES, LANES), 0)
           < lax.broadcasted_iota(jnp.int32, (LANES, LANES), 1)).astype(jnp.bfloat16)
    aff_t = jnp.swapaxes(aff.reshape(b, t, N_EXPERTS), 1, 2)
    idx, gate = _route_call(aff_t, tri, cap, _pick(cap, (256, 128, 64, 32, 16, 8)))
    idx_smem = idx.reshape(b * N_EXPERTS, 1, cap)

    y = _expert_call(idx_smem, h2, b, gate, expert_w_gate.astype(MXU_DTYPE),
                     expert_w_up.astype(MXU_DTYPE), expert_w_down.astype(MXU_DTYPE), cap)
    return _combine_call(idx_smem, x1.reshape(b, t, d), y, cap)


def kernel(x, positions, attn_norm_w, w_in, b_gate, mla_q_norm_w, mla_w_uq, mla_kv_norm_w, mla_w_ukv,
           mla_q_hnorm_w, mla_k_hnorm_w, diff_q_hnorm_w, diff_k_hnorm_w, diff_lambda, diff_subln_w, w_out,
           ffn_norm_w, router_w, expert_w_gate, expert_w_up, expert_w_down):
    for l in range(attn_norm_w.shape[0]):
        x = _layer(x, positions, l, attn_norm_w[l], w_in[l], b_gate[l], mla_q_norm_w[l], mla_w_uq[l],
                   mla_kv_norm_w[l], mla_w_ukv[l], mla_q_hnorm_w[l], mla_k_hnorm_w[l], diff_q_hnorm_w[l],
                   diff_k_hnorm_w[l], diff_lambda[l], diff_subln_w[l], w_out[l], ffn_norm_w[l], router_w[l],
                   expert_w_gate[l], expert_w_up[l], expert_w_down[l])
    return x
```

```python
import functools
import math

import jax
import jax.numpy as jnp
from jax import lax
from jax.experimental import pallas as pl
from jax.experimental.pallas import tpu as pltpu

F32 = jnp.float32
MXU_DTYPE = jnp.bfloat16

D_MODEL = 1024
MLA_HEADS = 16
MLA_Q_RANK = 256
MLA_KV_RANK = 256
MLA_NOPE_DIM = 64
MLA_ROPE_DIM = 32
MLA_QK_DIM = MLA_NOPE_DIM + MLA_ROPE_DIM
MLA_V_DIM = D_MODEL // MLA_HEADS
ROPE_THETA = 10000.0
DIFF_HEADS = 8
DIFF_QK_DIM = 64
DIFF_V_DIM = D_MODEL // DIFF_HEADS
N_EXPERTS = 16
CAPACITY_FACTOR = 2
EXPERT_FF = 2 * D_MODEL
EPS = 1e-6
LOG2E = math.log2(math.e)

LANES = 128
SUBLANES = 8
VMEM_LIMIT = 56 * 1024 * 1024

_SEG_CQ = (0, 256)
_SEG_CKV = (256, 512)
_SEG_KR = (512, 640)
_SEG_KRR = (640, 768)
_SEG_DQ = (768, 1792)
_SEG_DK = (1792, 2816)
_SEG_DV = (2816, 3840)
_SEG_GA = (3840, 4864)
_SEG_GB = (4864, 5888)
_IN_WIDTH_PACKED = 5888
_HEAD_GROUP = 4


def _rms(x, w):
    return x * lax.rsqrt(jnp.mean(x * x, axis=-1, keepdims=True) + EPS) * w


def _sigmoid(x):
    return 1.0 / (1.0 + jnp.exp(-x))


def _fold_lanes(x, op):
    parts = [x[:, i * LANES:(i + 1) * LANES] for i in range(x.shape[1] // LANES)]
    return functools.reduce(op, parts)


def _const_spec(shape):
    return pl.BlockSpec(shape, lambda *_: (0,) * len(shape), pipeline_mode=pl.Buffered(1))


def _proj_kernel(x_ref, anw_ref, win_ref, qnw_ref, wuq_ref, kvnw_ref, wukv_ref, qhw_ref, khw_ref,
                 dqw_ref, dkw_ref, bg_ref, rc_ref, rs_ref,
                 q_out, k_out, v_out, qd_out, kd_out, vd_out, ga_out, gb_out):
    x = x_ref[...]
    hb = _rms(x, anw_ref[...]).astype(MXU_DTYPE)

    def seg(ab):
        return jnp.dot(hb, win_ref[:, ab[0]:ab[1]], preferred_element_type=F32)

    rc, rs = rc_ref[...], rs_ref[...]
    qhw, khw = qhw_ref[...], khw_ref[...]
    cos_q, sin_q = rc * qhw[0:1], rs * qhw[1:2]
    cos_k, sin_k = rc * khw[0:1], rs * khw[1:2]

    def head_norm(v):
        return lax.rsqrt(jnp.sum(v * v, axis=-1, keepdims=True) * (1.0 / MLA_QK_DIM) + EPS)

    cqn = _rms(seg(_SEG_CQ), qnw_ref[...]).astype(MXU_DTYPE)
    gw = _HEAD_GROUP * 2 * LANES
    for grp in range(MLA_HEADS // _HEAD_GROUP):
        blk = jnp.dot(cqn, wuq_ref[:, grp * gw:(grp + 1) * gw], preferred_element_type=F32)
        for hh in range(_HEAD_GROUP):
            qh = blk[:, (2 * hh) * LANES:(2 * hh + 1) * LANES]
            qp = blk[:, (2 * hh + 1) * LANES:(2 * hh + 2) * LANES]
            q_out[grp * _HEAD_GROUP + hh] = ((qh * cos_q + qp * sin_q) * head_norm(qh)).astype(q_out.dtype)

    ckvn = _rms(seg(_SEG_CKV), kvnw_ref[...]).astype(MXU_DTYPE)
    kr = seg(_SEG_KR)
    krp = seg(_SEG_KRR) * sin_k
    v_out[...] = jnp.dot(ckvn, wukv_ref[:, MLA_HEADS * LANES:],
                         preferred_element_type=F32).astype(v_out.dtype)
    gw = _HEAD_GROUP * LANES
    for grp in range(MLA_HEADS // _HEAD_GROUP):
        blk = jnp.dot(ckvn, wukv_ref[:, grp * gw:(grp + 1) * gw], preferred_element_type=F32)
        for hh in range(_HEAD_GROUP):
            kh = blk[:, hh * LANES:(hh + 1) * LANES] + kr
            k_out[grp * _HEAD_GROUP + hh] = ((kh * cos_k + krp) * head_norm(kh)).astype(k_out.dtype)

    lo = lax.broadcasted_iota(jnp.int32, (1, LANES), 1) < DIFF_QK_DIM

    def halfnorm(blk, w):
        sq = blk * blk
        ss_lo = jnp.sum(jnp.where(lo, sq, 0.0), axis=-1, keepdims=True)
        ss_hi = jnp.sum(jnp.where(lo, 0.0, sq), axis=-1, keepdims=True)
        r = jnp.where(lo, lax.rsqrt(ss_lo * (1.0 / DIFF_QK_DIM) + EPS),
                      lax.rsqrt(ss_hi * (1.0 / DIFF_QK_DIM) + EPS))
        return blk * r * w

    dq = seg(_SEG_DQ)
    dqw = dqw_ref[...]
    for h in range(DIFF_HEADS):
        qn = halfnorm(dq[:, h * LANES:(h + 1) * LANES], dqw)
        qd_out[h, 0] = jnp.where(lo, qn, 0.0).astype(qd_out.dtype)
        qd_out[h, 1] = jnp.where(lo, 0.0, qn).astype(qd_out.dtype)
    dk = seg(_SEG_DK)
    dkw = dkw_ref[...]
    for h in range(DIFF_HEADS):
        kd_out[h] = halfnorm(dk[:, h * LANES:(h + 1) * LANES], dkw).astype(kd_out.dtype)
    vd_out[...] = seg(_SEG_DV).astype(vd_out.dtype)

    bg = bg_ref[...]
    ga_out[...] = _sigmoid(seg(_SEG_GA) + bg[:, :D_MODEL]).astype(ga_out.dtype)
    gb_out[...] = _sigmoid(seg(_SEG_GB) + bg[:, D_MODEL:]).astype(gb_out.dtype)


def _proj_call(x2d, b, t, anw, win, qnw, wuq, kvnw, wukv, qhw, khw, dqw, dkw, bg, rc, rs, tm):
    n = b * t
    tpb = t // tm
    row = lambda i: (i, 0)
    head = lambda i: (i // tpb, 0, i % tpb, 0)
    act = lambda: jax.ShapeDtypeStruct((n, D_MODEL), MXU_DTYPE)
    hd = lambda nh: jax.ShapeDtypeStruct((b, nh, t, LANES), MXU_DTYPE)
    return pl.pallas_call(
        _proj_kernel,
        grid=(n // tm,),
        in_specs=[
            pl.BlockSpec((tm, D_MODEL), row),
            _const_spec((1, D_MODEL)),
            _const_spec((D_MODEL, _IN_WIDTH_PACKED)),
            _const_spec((1, MLA_Q_RANK)),
            _const_spec((MLA_Q_RANK, MLA_HEADS * 2 * LANES)),
            _const_spec((1, MLA_KV_RANK)),
            _const_spec((MLA_KV_RANK, MLA_HEADS * LANES + D_MODEL)),
            _const_spec((2, LANES)), _const_spec((2, LANES)),
            _const_spec((1, LANES)), _const_spec((1, LANES)),
            _const_spec((1, 2 * D_MODEL)),
            pl.BlockSpec((tm, LANES), row), pl.BlockSpec((tm, LANES), row),
        ],
        out_specs=[
            pl.BlockSpec((None, MLA_HEADS, tm, LANES), head),
            pl.BlockSpec((None, MLA_HEADS, tm, LANES), head),
            pl.BlockSpec((tm, D_MODEL), row),
            pl.BlockSpec((None, DIFF_HEADS, 2, tm, LANES), lambda i: (i // tpb, 0, 0, i % tpb, 0)),
            pl.BlockSpec((None, DIFF_HEADS, tm, LANES), head),
            pl.BlockSpec((tm, D_MODEL), row),
            pl.BlockSpec((tm, D_MODEL), row),
            pl.BlockSpec((tm, D_MODEL), row),
        ],
        out_shape=[hd(MLA_HEADS), hd(MLA_HEADS), act(),
                   jax.ShapeDtypeStruct((b, DIFF_HEADS, 2, t, LANES), MXU_DTYPE), hd(DIFF_HEADS),
                   act(), act(), act()],
        compiler_params=pltpu.CompilerParams(dimension_semantics=("arbitrary",),
                                             vmem_limit_bytes=VMEM_LIMIT),
        name="proj",
    )(x2d, anw, win, qnw, wuq, kvnw, wukv, qhw, khw, dqw, dkw, bg, rc, rs)


_NT = (((1,), (1,)), ((), ()))


def _fold_rows(x, op):
    parts = [x[i * SUBLANES:(i + 1) * SUBLANES, :] for i in range(x.shape[0] // SUBLANES)]
    return functools.reduce(op, parts)


def _attn_step(g, q, k_ref, v_ref, s_ref, mx_ref, ls_ref, acc_ref, tk, finish, penalty_fn=None, *,
               keys_on_rows):
    nk = k_ref.shape[0] // tk
    key_axis = 0 if keys_on_rows else 1
    fold = _fold_rows if keys_on_rows else _fold_lanes
    all_ = slice(None)
    keys = (lambda off: (pl.ds(off, tk), all_)) if keys_on_rows else (lambda off: (all_, pl.ds(off, tk)))

    @pl.when(g == 0)
    def _():
        s_ref[1][...] = jnp.zeros_like(s_ref[1])
        mx_ref[1][...] = jnp.zeros_like(mx_ref[1])
        acc_ref[0][...] = jnp.zeros_like(acc_ref[0])
        ls_ref[0][...] = jnp.ones_like(ls_ref[0])

    def run(p):
        s_cur, s_prv, mx_cur = s_ref[p], s_ref[1 - p], mx_ref[p]
        ls_prv, acc_prv, ls_old, acc_old = ls_ref[1 - p], acc_ref[1 - p], ls_ref[p], acc_ref[p]
        m_prev = jnp.max(mx_ref[1 - p][...], axis=key_axis, keepdims=True)
        mx_cur[...] = jnp.full(mx_cur.shape, -jnp.inf, F32)
        ls_prv[...] = jnp.zeros_like(ls_prv)
        acc_prv[...] = jnp.zeros_like(acc_prv)

        def body(kc, _):
            off = pl.multiple_of(kc * tk, tk)
            kblk = k_ref[pl.ds(off, tk), :]
            lhs, rhs = (kblk, q) if keys_on_rows else (q, kblk)
            s = lax.dot_general(lhs, rhs, _NT, preferred_element_type=F32)
            if penalty_fn is not None:
                s = s - penalty_fn(off)
            s_cur[keys(off)] = s
            mx_cur[...] = jnp.maximum(mx_cur[...], fold(s, jnp.maximum))

            e = jnp.exp2(s_prv[keys(off)] - m_prev)
            ls_prv[...] += fold(e, jnp.add)
            eb = e.astype(MXU_DTYPE)
            if keys_on_rows:
                acc_prv[...] += jnp.dot(v_ref[:, pl.ds(off, tk)], eb, preferred_element_type=F32)
            else:
                acc_prv[...] += jnp.dot(eb, v_ref[pl.ds(off, tk), :], preferred_element_type=F32)

            def normalised(first, count):
                sl = (all_, pl.ds(first, count)) if keys_on_rows else (pl.ds(first, count), all_)
                o = acc_old[sl] / jnp.sum(ls_old[sl], axis=key_axis, keepdims=True)
                return o.T if keys_on_rows else o

            finish(p, kc, nk, normalised)
            return 0

        lax.fori_loop(0, nk, body, 0, unroll=_pick(nk, (4, 2)))

    for par in range(2):
        @pl.when(g % 2 == par)
        def _():
            run(par)


def _attn_scratch(tq, t, keys_on_rows):
    if keys_on_rows:
        shapes = [(t, tq), (SUBLANES, tq), (SUBLANES, tq), (LANES, tq)]
    else:
        shapes = [(tq, t), (tq, LANES), (tq, LANES), (tq, LANES)]
    return [[pltpu.VMEM(shape, F32)] * 2 for shape in shapes]


def _attn_units(units):
    stage = lambda d: (lambda g: jnp.clip(g - d, 0, units - 1))
    return stage(0), stage(1), stage(2)


def _mla_kernel(q_ref, k_ref, v_ref, o_ref, s_ref, mx_ref, ls_ref, acc_ref, *, tk):
    def finish(p, kc, nk, normalised):
        rows = o_ref.shape[0] // nk
        r0 = pl.multiple_of(kc * rows, rows)
        sl = slice(p * MLA_V_DIM, (p + 1) * MLA_V_DIM)
        o_ref[pl.ds(r0, rows), sl] = normalised(r0, rows)[:, sl].astype(o_ref.dtype)

    _attn_step(pl.program_id(0), q_ref[...], k_ref, v_ref, s_ref, mx_ref, ls_ref, acc_ref, tk, finish,
               keys_on_rows=True)


def _mla_call(q, k, v, tq, tk):
    b, nh, t, _ = q.shape
    nq = t // tq
    units = b * nh * nq
    cur, prev, old = _attn_units(units)

    def unit(u):
        return u // (nh * nq), (u // (2 * nq)) % (nh // 2), (u // 2) % nq, u % 2

    def q_map(g):
        bi, p, i, h = unit(cur(g))
        return bi, 2 * p + h, i, 0

    def k_map(g):
        bi, p, _, h = unit(cur(g))
        return bi, 2 * p + h, 0, 0

    def v_map(g):
        bi, p, _, _ = unit(prev(g))
        return bi, p, 0

    def o_map(g):
        bi, p, i, _ = unit(old(g))
        return bi, i, p

    return pl.pallas_call(
        functools.partial(_mla_kernel, tk=tk),
        grid=(units + 2,),
        in_specs=[
            pl.BlockSpec((None, None, tq, LANES), q_map),
            pl.BlockSpec((None, None, t, LANES), k_map),
            pl.BlockSpec((None, LANES, t), v_map),
        ],
        out_specs=pl.BlockSpec((None, tq, LANES), o_map),
        out_shape=jax.ShapeDtypeStruct((b, t, D_MODEL), MXU_DTYPE),
        scratch_shapes=_attn_scratch(tq, t, True),
        compiler_params=pltpu.CompilerParams(dimension_semantics=("arbitrary",),
                                             vmem_limit_bytes=VMEM_LIMIT),
        name="mla_attn",
    )(q, k, v)


_MAP_GROUP = 16


def _diff_kernel(sc_ref, q_ref, k_ref, v_ref, pq_ref, pk_ref, w_ref, o_ref,
                 s_ref, mx_ref, ls_ref, acc_ref, *, tk, nq, units, out_scale):
    g = pl.program_id(0)
    tq = o_ref.shape[0]
    h_cur = (jnp.minimum(g, units - 1) // nq) % DIFF_HEADS
    slope = sc_ref[h_cur]
    lam = sc_ref[DIFF_HEADS]
    pq = pq_ref[...] * slope
    w = w_ref[...]

    def both_maps(a, b):
        n, c = a.shape
        grouped = lambda x: x.reshape(n // _MAP_GROUP, 1, _MAP_GROUP, c)
        return jnp.concatenate([grouped(a), grouped(b)], axis=1).reshape(2 * n, c)

    def penalty(off):
        pen = jnp.abs(pq - pk_ref[:, pl.ds(off, tk)] * slope)
        return both_maps(pen, pen)

    def finish(p, kc, nk, normalised):
        rows = tq // nk
        r0 = pl.multiple_of(kc * rows, rows)
        o = normalised(2 * r0, 2 * rows).reshape(rows // _MAP_GROUP, 2, _MAP_GROUP, LANES)
        od = (o[:, 0] - lam * o[:, 1]).reshape(rows, LANES)
        o_ref[pl.ds(r0, rows), :] = (_rms(od, w) * out_scale).astype(o_ref.dtype)

    q = both_maps(q_ref[0], q_ref[1])
    _attn_step(g, q, k_ref, v_ref, s_ref, mx_ref, ls_ref, acc_ref, tk, finish, penalty, keys_on_rows=False)


def _diff_call(scalars, qd, k, v, posq, posk, subw, tq, tk, out_scale):
    b, nh, _, t, _ = qd.shape
    nq = t // tq
    units = b * nh * nq
    cur, prev, old = _attn_units(units)

    def unit(u):
        return u // (nh * nq), (u // nq) % nh, u % nq

    def q_map(g):
        bi, h, i = unit(cur(g))
        return bi, h, 0, i, 0

    def k_map(g):
        bi, h, _ = unit(cur(g))
        return bi, h, 0, 0

    def v_map(g):
        bi, h, _ = unit(prev(g))
        return bi, 0, h

    def o_map(g):
        bi, h, i = unit(old(g))
        return bi, i, h

    return pl.pallas_call(
        functools.partial(_diff_kernel, tk=tk, nq=nq, units=units, out_scale=out_scale),
        grid=(units + 2,),
        in_specs=[
            pl.BlockSpec(memory_space=pltpu.SMEM),
            pl.BlockSpec((None, None, 2, tq, LANES), q_map),
            pl.BlockSpec((None, None, t, LANES), k_map),
            pl.BlockSpec((None, t, LANES), v_map),
            pl.BlockSpec((None, tq, 1), lambda g: (unit(cur(g))[0], unit(cur(g))[2], 0)),
            pl.BlockSpec((None, 1, t), lambda g: (unit(cur(g))[0], 0, 0)),
            pl.BlockSpec((1, LANES), lambda g: (0, 0)),
        ],
        out_specs=pl.BlockSpec((None, tq, LANES), o_map),
        out_shape=jax.ShapeDtypeStruct((b, t, D_MODEL), MXU_DTYPE),
        scratch_shapes=_attn_scratch(2 * tq, t, False),
        compiler_params=pltpu.CompilerParams(dimension_semantics=("arbitrary",),
                                             vmem_limit_bytes=VMEM_LIMIT),
        name="diff_attn",
    )(scalars, qd, k, v, posq, posk, subw)


def _split2(a):
    a1 = a.astype(jnp.bfloat16)
    return a1, (a - a1.astype(F32)).astype(jnp.bfloat16)


def _merge_kernel(x_ref, om_ref, od_ref, ga_ref, gb_ref, wo_ref, fnw_ref, rw1_ref, rw2_ref,
                  x1_out, aff_out, h2_out):
    merged = (ga_ref[...].astype(F32) * om_ref[...].astype(F32)
              + gb_ref[...].astype(F32) * od_ref[...].astype(F32))
    x1 = x_ref[...] + jnp.dot(merged.astype(MXU_DTYPE), wo_ref[...], preferred_element_type=F32)
    x1_out[...] = x1
    h2 = _rms(x1, fnw_ref[...])

    for c in range(D_MODEL // LANES):
        h2_out[pl.ds(c, h2.shape[0], stride=SUBLANES), :] = h2[:, c * LANES:(c + 1) * LANES]

    h1, hh2 = _split2(h2)
    w1, w2 = rw1_ref[...], rw2_ref[...]
    dot = lambda a, w: jnp.dot(a, w, preferred_element_type=F32)
    logits = (dot(hh2, w1) + dot(h1, w2)) + dot(h1, w1)
    lane = lax.broadcasted_iota(jnp.int32, (1, LANES), 1)
    logits = jnp.where(lane < N_EXPERTS, logits, -jnp.inf)
    e = jnp.exp(logits - jnp.max(logits, axis=-1, keepdims=True))
    aff = e / jnp.sum(e, axis=-1, keepdims=True)
    aff_out[...] = aff[:, :N_EXPERTS]


def _merge_call(x2d, om, od, ga, gb, wo, fnw, rw1, rw2, tm):
    n = x2d.shape[0]
    row = lambda i: (i, 0)
    act = pl.BlockSpec((tm, D_MODEL), row)
    return pl.pallas_call(
        _merge_kernel,
        grid=(n // tm,),
        in_specs=[act, act, act, act, act,
                  _const_spec((D_MODEL, D_MODEL)), _const_spec((1, D_MODEL)),
                  _const_spec((D_MODEL, LANES)), _const_spec((D_MODEL, LANES))],
        out_specs=[act, pl.BlockSpec((tm, N_EXPERTS), row), pl.BlockSpec((tm * SUBLANES, LANES), row)],
        out_shape=[jax.ShapeDtypeStruct((n, D_MODEL), F32),
                   jax.ShapeDtypeStruct((n, N_EXPERTS), F32),
                   jax.ShapeDtypeStruct((n * SUBLANES, LANES), F32)],
        compiler_params=pltpu.CompilerParams(dimension_semantics=("arbitrary",),
                                             vmem_limit_bytes=VMEM_LIMIT),
        name="merge",
    )(x2d, om, od, ga, gb, wo, fnw, rw1, rw2)


def _route_kernel(a_ref, tri_ref, idx_out, gate_out, rank_ref, *, cap, jc):
    a = a_ref[...]
    ne, t = a.shape

    def search(_, lohi):
        lo, hi = lohi
        mid = lo + ((hi - lo + 1) >> 1)
        cnt = jnp.sum(jnp.where(a >= pltpu.bitcast(mid, F32), 1.0, 0.0), axis=-1, keepdims=True)
        ok = cnt >= cap
        return jnp.where(ok, mid, lo), jnp.where(ok, hi, mid - 1)

    lo0 = jnp.zeros((ne, 1), jnp.int32)
    hi0 = jnp.full((ne, 1), 0x7F800000, jnp.int32)
    thr_bits, _ = lax.fori_loop(0, 32, search, (lo0, hi0))
    thr = pltpu.bitcast(thr_bits, F32)

    tri = tri_ref[...]

    def prefix(mask):
        carry = jnp.zeros((ne, 1), F32)
        cols = []
        for c in range(t // LANES):
            m = mask[:, c * LANES:(c + 1) * LANES]
            cols.append(jnp.dot(m.astype(jnp.bfloat16), tri, preferred_element_type=F32) + carry)
            carry = carry + jnp.sum(m, axis=-1, keepdims=True)
        return jnp.concatenate(cols, axis=1), carry

    gt = jnp.where(a > thr, 1.0, 0.0)
    eq = jnp.where(a == thr, 1.0, 0.0)
    pre_eq, _ = prefix(eq)
    need = cap - jnp.sum(gt, axis=-1, keepdims=True)
    sel = jnp.maximum(gt, jnp.where(pre_eq < need, eq, 0.0))
    rank, _ = prefix(sel)
    rank_ref[...] = jnp.where(sel > 0.0, rank, -1.0)

    tok = lax.broadcasted_iota(jnp.int32, (1, t), 1).astype(F32)

    def per_expert(e, _):
        r = rank_ref[pl.ds(e, 1), :]
        av = a_ref[pl.ds(e, 1), :]

        def per_slots(c, _):
            j0 = pl.multiple_of(c * jc, jc)
            slot = (lax.broadcasted_iota(jnp.int32, (jc, 1), 0) + j0).astype(F32)
            hit = r == slot
            idx_out[e, pl.ds(j0, jc), :] = jnp.sum(jnp.where(hit, tok, 0.0), axis=-1,
                                                   keepdims=True).astype(jnp.int32)
            gate_out[e, pl.ds(j0, jc), :] = jnp.sum(jnp.where(hit, av, 0.0), axis=-1, keepdims=True)
            return 0

        return lax.fori_loop(0, cap // jc, per_slots, 0)

    lax.fori_loop(0, ne, per_expert, 0)


def _route_call(aff_t, tri, cap, jc):
    b, ne, t = aff_t.shape
    return pl.pallas_call(
        functools.partial(_route_kernel, cap=cap, jc=jc),
        grid=(b,),
        in_specs=[pl.BlockSpec((None, ne, t), lambda bi: (bi, 0, 0)), _const_spec((LANES, LANES))],
        out_specs=[pl.BlockSpec((None, ne, cap, 1), lambda bi: (bi, 0, 0, 0)),
                   pl.BlockSpec((None, ne, cap, 1), lambda bi: (bi, 0, 0, 0))],
        out_shape=[jax.ShapeDtypeStruct((b, ne, cap, 1), jnp.int32),
                   jax.ShapeDtypeStruct((b, ne, cap, 1), F32)],
        scratch_shapes=[pltpu.VMEM((ne, t), F32)],
        compiler_params=pltpu.CompilerParams(dimension_semantics=("arbitrary",),
                                             vmem_limit_bytes=VMEM_LIMIT),
        name="route",
    )(aff_t, tri)


_ROW_UNROLL = 8


_FF_CHUNK = 512


def _expert_kernel(idx_ref, idxn_ref, h2_hbm, gate_ref, wg_ref, wu_ref, wd_ref, y_out,
                   xe_ref, hid_ref, sem, *, cap, seq):
    nb = pl.num_programs(1)
    step = pl.program_id(0) * nb + pl.program_id(1)
    last = pl.num_programs(0) * nb - 1
    cur = step % 2
    nxt = 1 - cur
    b_next = jnp.minimum(step + 1, last) % nb
    tile = lambda i: pl.ds(pl.multiple_of(i * SUBLANES, SUBLANES), SUBLANES)

    def row_copy(table_ref, b_src, slot, j):
        return pltpu.make_async_copy(h2_hbm.at[tile(b_src * seq + table_ref[0, j]), :],
                                     xe_ref.at[slot, tile(j), :], sem.at[slot])

    def wait_slot(slot):
        pltpu.make_async_copy(xe_ref.at[slot], xe_ref.at[slot], sem.at[slot]).wait()

    @pl.when(step == 0)
    def _():
        def first(j, _):
            row_copy(idx_ref, pl.program_id(1), 0, j).start()
            return 0

        lax.fori_loop(0, cap, first, 0)

    for j in range(cap):
        row_copy(idxn_ref, b_next, nxt, j).start(priority=j % 2)

    wait_slot(cur)
    xe = jnp.concatenate([xe_ref[cur, pl.ds(c, cap, stride=SUBLANES), :] for c in range(D_MODEL // LANES)],
                         axis=1).astype(MXU_DTYPE)
    for c in range(EXPERT_FF // _FF_CHUNK):
        cols = slice(c * _FF_CHUNK, (c + 1) * _FF_CHUNK)
        g = jnp.dot(xe, wg_ref[:, cols], preferred_element_type=F32)
        u = jnp.dot(xe, wu_ref[:, cols], preferred_element_type=F32)
        hid_ref[:, cols] = (g * _sigmoid(g) * u).astype(MXU_DTYPE)
    y = jnp.dot(hid_ref[...], wd_ref[...], preferred_element_type=F32) * gate_ref[...]
    for c in range(D_MODEL // LANES):
        y_out[pl.ds(c, cap, stride=SUBLANES), :] = y[:, c * LANES:(c + 1) * LANES]

    @pl.when(step == last)
    def _():
        wait_slot(nxt)


def _expert_call(idx_smem, h2, b, gate, wg, wu, wd, cap):
    t = h2.shape[0] // (b * SUBLANES)
    ne = wg.shape[0]

    def next_idx(e, bi):
        s = jnp.minimum(e * b + bi + 1, ne * b - 1)
        return (s % b) * ne + s // b, 0, 0

    return pl.pallas_call(
        functools.partial(_expert_kernel, cap=cap, seq=t),
        grid=(ne, b),
        in_specs=[
            pl.BlockSpec((None, 1, cap), lambda e, bi: (bi * ne + e, 0, 0), memory_space=pltpu.SMEM),
            pl.BlockSpec((None, 1, cap), next_idx, memory_space=pltpu.SMEM),
            pl.BlockSpec(memory_space=pl.ANY),
            pl.BlockSpec((None, None, cap, 1), lambda e, bi: (bi, e, 0, 0)),
            pl.BlockSpec((None, D_MODEL, EXPERT_FF), lambda e, bi: (e, 0, 0)),
            pl.BlockSpec((None, D_MODEL, EXPERT_FF), lambda e, bi: (e, 0, 0)),
            pl.BlockSpec((None, EXPERT_FF, D_MODEL), lambda e, bi: (e, 0, 0)),
        ],
        out_specs=pl.BlockSpec((None, None, cap * SUBLANES, LANES), lambda e, bi: (bi, e, 0, 0)),
        out_shape=jax.ShapeDtypeStruct((b, ne, cap * SUBLANES, LANES), F32),
        scratch_shapes=[pltpu.VMEM((2, cap * SUBLANES, LANES), F32), pltpu.VMEM((cap, EXPERT_FF), MXU_DTYPE),
                        pltpu.SemaphoreType.DMA((2,))],
        compiler_params=pltpu.CompilerParams(dimension_semantics=("arbitrary",) * 2,
                                             vmem_limit_bytes=VMEM_LIMIT),
        name="experts",
    )(idx_smem, idx_smem, h2, gate, wg, wu, wd)


def _combine_kernel(idx_ref, x1_hbm, y_ref, o_hbm, acc_ref, stage_ref, sem, *, cap):
    bi, e = pl.program_id(0), pl.program_id(1)
    tile = lambda i: pl.ds(pl.multiple_of(i * SUBLANES, SUBLANES), SUBLANES)

    last = e == pl.num_programs(1) - 1
    rows = stage_ref.shape[1]
    n_blocks = acc_ref.shape[0] // (SUBLANES * rows)

    def load(r):
        return pltpu.make_async_copy(x1_hbm.at[bi, pl.ds(r * rows, rows), :], stage_ref.at[r % 2],
                                     sem.at[0, r % 2])

    def store(r):
        return pltpu.make_async_copy(stage_ref.at[r % 2], o_hbm.at[bi, pl.ds(r * rows, rows), :],
                                     sem.at[1, r % 2])

    @pl.when(e == 0)
    def _():
        acc_ref[...] = jnp.zeros_like(acc_ref)

    @pl.when(last)
    def _():
        load(0).start()

    def scatter(g, _):
        toks = [idx_ref[0, g * _ROW_UNROLL + u] for u in range(_ROW_UNROLL)]
        sums = [acc_ref[tile(toks[u]), :] + y_ref[tile(g * _ROW_UNROLL + u), :] for u in range(_ROW_UNROLL)]
        for u in range(_ROW_UNROLL):
            acc_ref[tile(toks[u]), :] = sums[u]
        return 0

    lax.fori_loop(0, cap // _ROW_UNROLL, scatter, 0)

    @pl.when(last)
    def _():
        for r in range(n_blocks):
            if r + 1 < n_blocks:
                if r >= 1:
                    store(r - 1).wait()
                load(r + 1).start()
            load(r).wait()
            for c in range(D_MODEL // LANES):
                stage_ref[r % 2, :, c * LANES:(c + 1) * LANES] += acc_ref[
                    pl.ds(r * rows * SUBLANES + c, rows, stride=SUBLANES), :]
            store(r).start()
        for r in range(max(n_blocks - 2, 0), n_blocks):
            store(r).wait()


def _combine_call(idx_smem, x1, y, cap):
    b, t, d = x1.shape
    ne = y.shape[1]
    return pl.pallas_call(
        functools.partial(_combine_kernel, cap=cap),
        grid=(b, ne),
        in_specs=[
            pl.BlockSpec((None, 1, cap), lambda bi, e: (bi * ne + e, 0, 0), memory_space=pltpu.SMEM),
            pl.BlockSpec(memory_space=pl.ANY),
            pl.BlockSpec((None, None, cap * SUBLANES, LANES), lambda bi, e: (bi, e, 0, 0)),
        ],
        out_specs=pl.BlockSpec(memory_space=pl.ANY),
        out_shape=jax.ShapeDtypeStruct((b, t, d), F32),
        scratch_shapes=[pltpu.VMEM((t * SUBLANES, LANES), F32),
                        pltpu.VMEM((2, _pick(t, (512, 256, 128)), d), F32),
                        pltpu.SemaphoreType.DMA((2, 2))],
        compiler_params=pltpu.CompilerParams(dimension_semantics=("arbitrary",) * 2,
                                             vmem_limit_bytes=VMEM_LIMIT),
        name="combine",
    )(idx_smem, x1, y)


def _pad_heads(w, n_heads, width):
    r = w.shape[0]
    w = w.reshape(r, n_heads, width)
    return jnp.pad(w, ((0, 0), (0, 0), (0, LANES - width))).reshape(r, n_heads * LANES)


def _pack_w_in(w_in):
    o = 0
    parts = {}
    for name, width in (("cq", MLA_Q_RANK), ("ckv", MLA_KV_RANK), ("kr", MLA_ROPE_DIM),
                        ("dq", D_MODEL), ("dk", D_MODEL), ("dv", D_MODEL), ("ga", D_MODEL), ("gb", D_MODEL)):
        parts[name] = w_in[:, o:o + width]
        o += width
    kr = jnp.pad(parts["kr"], ((0, 0), (MLA_NOPE_DIM, LANES - MLA_QK_DIM)))
    return jnp.concatenate([parts["cq"], parts["ckv"], kr, _rope_partner(kr), parts["dq"], parts["dk"],
                            parts["dv"], parts["ga"], parts["gb"]], axis=1).astype(MXU_DTYPE)


def _rope_partner(w):
    half = MLA_ROPE_DIM // 2
    lane = jnp.arange(LANES)
    src = jnp.where((lane >= MLA_NOPE_DIM) & (lane < MLA_NOPE_DIM + half), lane + half,
                    jnp.where((lane >= MLA_NOPE_DIM + half) & (lane < MLA_QK_DIM), lane - half, lane))
    blocks = w.reshape(w.shape[:-1] + (w.shape[-1] // LANES, LANES))
    return jnp.take(blocks, src, axis=-1).reshape(w.shape)


def _rope_tables(positions):
    inv_freq = 1.0 / (ROPE_THETA ** (jnp.arange(0, MLA_ROPE_DIM, 2, dtype=F32) / MLA_ROPE_DIM))
    ang = positions.astype(F32).reshape(-1, 1) * inv_freq
    cos, sin = jnp.cos(ang), jnp.sin(ang)
    n = ang.shape[0]
    z = lambda w: jnp.zeros((n, w), F32)
    tail = LANES - MLA_QK_DIM
    rc = jnp.concatenate([jnp.ones((n, MLA_NOPE_DIM), F32), cos, cos, z(tail)], axis=1)
    rs = jnp.concatenate([z(MLA_NOPE_DIM), -sin, sin, z(tail)], axis=1)
    return rc, rs


def _pick(t, prefs):
    for p in prefs:
        if t % p == 0:
            return p
    return t


def _layer(x, positions, layer_idx, attn_norm_w, w_in, b_gate, mla_q_norm_w, mla_w_uq, mla_kv_norm_w,
           mla_w_ukv, mla_q_hnorm_w, mla_k_hnorm_w, diff_q_hnorm_w, diff_k_hnorm_w, diff_lambda,
           diff_subln_w, w_out, ffn_norm_w, router_w, expert_w_gate, expert_w_up, expert_w_down):
    b, t, d = x.shape
    n = b * t
    cap = CAPACITY_FACTOR * t // N_EXPERTS
    row = lambda v: v.reshape(1, -1).astype(F32)

    win = _pack_w_in(w_in)
    wuq = _pad_heads(mla_w_uq, MLA_HEADS, MLA_QK_DIM)
    wuq = jnp.stack([wuq.reshape(MLA_Q_RANK, MLA_HEADS, LANES),
                     _rope_partner(wuq).reshape(MLA_Q_RANK, MLA_HEADS, LANES)], axis=2)
    wuq = wuq.reshape(MLA_Q_RANK, MLA_HEADS * 2 * LANES).astype(MXU_DTYPE)
    wukv = mla_w_ukv.reshape(MLA_KV_RANK, MLA_HEADS, MLA_NOPE_DIM + MLA_V_DIM)
    wuk = _pad_heads(wukv[:, :, :MLA_NOPE_DIM].reshape(MLA_KV_RANK, -1), MLA_HEADS, MLA_NOPE_DIM)
    wuv = wukv[:, :, MLA_NOPE_DIM:].reshape(MLA_KV_RANK, -1)
    wukv_p = jnp.concatenate([wuk, wuv], axis=1).astype(MXU_DTYPE)
    pad_gain = lambda w: jnp.pad(w.astype(F32), (0, LANES - MLA_QK_DIM)).reshape(1, LANES)
    qhw = pad_gain(mla_q_hnorm_w) * (MLA_QK_DIM ** -0.5 * LOG2E)
    khw = pad_gain(mla_k_hnorm_w)
    qhw = jnp.concatenate([qhw, _rope_partner(qhw)], axis=0)
    khw = jnp.concatenate([khw, _rope_partner(khw)], axis=0)
    dqw = jnp.tile(diff_q_hnorm_w.astype(F32), 2).reshape(1, LANES) * (DIFF_QK_DIM ** -0.5 * LOG2E)
    dkw = jnp.tile(diff_k_hnorm_w.astype(F32), 2).reshape(1, LANES)
    rc, rs = _rope_tables(positions)

    tm = _pick(t, (512, 256, 128))
    q, k, v, qd, kd, vd, ga, gb = _proj_call(
        x.reshape(n, d), b, t, row(attn_norm_w), win, row(mla_q_norm_w), wuq, row(mla_kv_norm_w), wukv_p,
        qhw, khw, dqw, dkw, row(b_gate), rc, rs, tm)

    tq = _pick(t, (1024, 512, 256, 128))
    tk = _pick(t, (512, 256, 128))
    o_mla = _mla_call(q, k, jnp.swapaxes(v.reshape(b, t, d), 1, 2), tq, tk)

    lam_init = 0.8 - 0.6 * math.exp(-0.3 * layer_idx)
    lf = diff_lambda.astype(F32)
    lam = jnp.exp(jnp.sum(lf[0] * lf[1])) - jnp.exp(jnp.sum(lf[2] * lf[3])) + lam_init
    slopes = jnp.asarray([2.0 ** (-8.0 * (i + 1) / DIFF_HEADS) * LOG2E for i in range(DIFF_HEADS)], F32)
    scalars = jnp.concatenate([slopes, lam.reshape(1)])
    posf = positions.astype(F32)
    o_diff = _diff_call(scalars, qd, kd, vd.reshape(b, t, d), posf.reshape(b, t, 1),
                        posf.reshape(b, 1, t), row(diff_subln_w), tq // 2, tk, 1.0 - lam_init)

    rw = jnp.pad(router_w.astype(F32), ((0, 0), (0, LANES - N_EXPERTS)))
    rw1, rw2 = _split2(rw)
    tm2 = _pick(t, (512, 256, 128))
    x1, aff, h2 = _merge_call(x.reshape(n, d), o_mla.reshape(n, d), o_diff.reshape(n, d), ga, gb,
                              w_out.astype(MXU_DTYPE), row(ffn_norm_w), rw1, rw2, tm2)

    tri = (lax.broadcasted_iota(jnp.int32, (LANES, LANES), 0)
           < lax.broadcasted_iota(jnp.int32, (LANES, LANES), 1)).astype(jnp.bfloat16)
    aff_t = jnp.swapaxes(aff.reshape(b, t, N_EXPERTS), 1, 2)
    idx, gate = _route_call(aff_t, tri, cap, _pick(cap, (256, 128, 64, 32, 16, 8)))
    idx_smem = idx.reshape(b * N_EXPERTS, 1, cap)

    y = _expert_call(idx_smem, h2, b, gate, expert_w_gate.astype(MXU_DTYPE),
                     expert_w_up.astype(MXU_DTYPE), expert_w_down.astype(MXU_DTYPE), cap)
    return _combine_call(idx_smem, x1.reshape(b, t, d), y, cap)


def kernel(x, positions, attn_norm_w, w_in, b_gate, mla_q_norm_w, mla_w_uq, mla_kv_norm_w, mla_w_ukv,
           mla_q_hnorm_w, mla_k_hnorm_w, diff_q_hnorm_w, diff_k_hnorm_w, diff_lambda, diff_subln_w, w_out,
           ffn_norm_w, router_w, expert_w_gate, expert_w_up, expert_w_down):
    for l in range(attn_norm_w.shape[0]):
        x = _layer(x, positions, l, attn_norm_w[l], w_in[l], b_gate[l], mla_q_norm_w[l], mla_w_uq[l],
                   mla_kv_norm_w[l], mla_w_ukv[l], mla_q_hnorm_w[l], mla_k_hnorm_w[l], diff_q_hnorm_w[l],
                   diff_k_hnorm_w[l], diff_lambda[l], diff_subln_w[l], w_out[l], ffn_norm_w[l], router_w[l],
                   expert_w_gate[l], expert_w_up[l], expert_w_down[l])
    return x
```
